```python
import math
import jax, jax.numpy as jnp
from jax import lax
import numpy as np

D_MODEL = 1024
BATCH = 8
SEQ = 8192
DEPTH = 1

CHUNK = 64
Q_BLOCK = 128
SB_HEADS = 8
SB_HEAD_DIM = 64
SB_WIDTH = SB_HEADS * SB_HEAD_DIM
DF_HEADS = 4
DF_HEAD_DIM = 64
DF_V_DIM = 2 * DF_HEAD_DIM
DF_QK_WIDTH = DF_HEADS * 2 * DF_HEAD_DIM
DF_WIDTH = DF_HEADS * DF_V_DIM
N_BUCKETS = 32
MAX_DISTANCE = 256
PEER_HEADS = 8
PEER_N_KEYS = 128
PEER_N_EXPERTS = PEER_N_KEYS * PEER_N_KEYS
PEER_QUERY_DIM = 256
PEER_HALF = PEER_QUERY_DIM // 2
PEER_TOPK = 16
PEER_TOKEN_BLOCK = 128
LN_EPS = 1e-5
ALPHA = (2.0 * DEPTH) ** 0.25
BETA = (8.0 * DEPTH) ** -0.25
COL_SIZES = (SB_WIDTH, SB_WIDTH, SB_WIDTH, DF_QK_WIDTH, DF_QK_WIDTH, DF_WIDTH, D_MODEL, D_MODEL)
COL_OFFS = tuple(int(v) for v in np.cumsum((0,) + COL_SIZES))
IN_COLS = COL_OFFS[-1]

kernel_name = "hybrid_stickbreak_diffattn_peer_deepnorm"


def layer_norm(x, g, b):
    xf = x.astype(jnp.float32)
    mu = jnp.mean(xf, axis=-1, keepdims=True)
    var = jnp.mean(jnp.square(xf - mu), axis=-1, keepdims=True)
    y = (xf - mu) * lax.rsqrt(var + LN_EPS) * g.astype(jnp.float32) + b.astype(jnp.float32)
    return y.astype(x.dtype)


def t5_bucket(rel):
    half = N_BUCKETS // 2
    max_exact = half // 2
    ret = (rel > 0).astype(jnp.int32) * half
    n = jnp.abs(rel)
    nf = jnp.maximum(n, 1).astype(jnp.float32)
    large = max_exact + (jnp.log(nf / max_exact) / math.log(MAX_DISTANCE / max_exact)
                         * (half - max_exact)).astype(jnp.int32)
    large = jnp.minimum(large, half - 1)
    return ret + jnp.where(n < max_exact, n, large)


def token_mixer(x, w_in, lq1, lk1, lq2, lk2, subln_g, rel_bias, w_a, w_b, w_o, layer_idx):
    B, S, _ = x.shape
    proj = x @ w_in
    cols = [proj[..., COL_OFFS[i]:COL_OFFS[i + 1]] for i in range(len(COL_SIZES))]
    sb_q, sb_k, sb_v = [c.reshape(B, S, SB_HEADS, SB_HEAD_DIM).transpose(0, 2, 1, 3) for c in cols[:3]]
    df_q, df_k = [c.reshape(B, S, DF_HEADS, 2, DF_HEAD_DIM).transpose(0, 2, 3, 1, 4) for c in cols[3:5]]
    df_v = cols[5].reshape(B, S, DF_HEADS, DF_V_DIM).transpose(0, 2, 1, 3)
    gate_a, gate_b = cols[6], cols[7]

    lam_init = 0.8 - 0.6 * math.exp(-0.3 * layer_idx)
    lam = (jnp.exp(jnp.sum(lq1.astype(jnp.float32) * lk1.astype(jnp.float32)))
           - jnp.exp(jnp.sum(lq2.astype(jnp.float32) * lk2.astype(jnp.float32))) + lam_init)
    sb_scale = 1.0 / math.sqrt(SB_HEAD_DIM)
    df_scale = 1.0 / math.sqrt(DF_HEAD_DIM)

    outs_sb, outs_df = [], []
    for i in range(S // Q_BLOCK):
        q0 = i * Q_BLOCK
        L = q0 + Q_BLOCK
        t = q0 + jnp.arange(Q_BLOCK, dtype=jnp.int32)
        s = jnp.arange(L, dtype=jnp.int32)
        z = jnp.einsum('bhqd,bhkd->bhqk', sb_q[:, :, q0:L], sb_k[:, :, :L]).astype(jnp.float32) * sb_scale
        strict = s[None, :] < t[:, None]
        log_fail = jnp.where(strict, jax.nn.log_sigmoid(-z), 0.0)
        after = lax.cumsum(log_fail, axis=3, reverse=True) - log_fail
        w_sb = jnp.where(strict, jnp.exp(jax.nn.log_sigmoid(z) + after), 0.0)
        outs_sb.append(jnp.einsum('bhqk,bhkd->bhqd', w_sb.astype(sb_v.dtype), sb_v[:, :, :L]))
        allowed = (s // CHUNK)[None, :] <= (t // CHUNK)[:, None]
        bias = rel_bias[t5_bucket(s[None, :] - t[:, None])].astype(jnp.float32).transpose(2, 0, 1)
        logits = jnp.einsum('bhmqd,bhmkd->bhmqk', df_q[:, :, :, q0:L], df_k[:, :, :, :L]).astype(jnp.float32) * df_scale
        logits = jnp.where(allowed, logits + bias[None, :, None], -jnp.inf)
        p = jax.nn.softmax(logits, axis=-1)
        a = p[:, :, 0] - lam * p[:, :, 1]
        outs_df.append(jnp.einsum('bhqk,bhkd->bhqd', a.astype(df_v.dtype), df_v[:, :, :L]))

    o_sb = jnp.concatenate(outs_sb, axis=2).transpose(0, 2, 1, 3).reshape(B, S, SB_WIDTH)
    o_df = jnp.concatenate(outs_df, axis=2).astype(jnp.float32)
    o_df = o_df * lax.rsqrt(jnp.mean(jnp.square(o_df), axis=-1, keepdims=True) + LN_EPS)
    o_df = (o_df * subln_g.astype(jnp.float32) * (1.0 - lam_init)).astype(x.dtype)
    o_df = o_df.transpose(0, 2, 1, 3).reshape(B, S, DF_WIDTH)
    merged = jax.nn.sigmoid(gate_a) * (o_sb @ w_a) + jax.nn.sigmoid(gate_b) * (o_df @ w_b)
    return merged @ w_o


def peer_ffn(h, wq, k1, k2, u, v):
    B, S, D = h.shape
    xt = h.reshape(-1, PEER_TOKEN_BLOCK, D)

    def block(xb):
        T = xb.shape[0]
        q = (xb @ wq).reshape(T, PEER_HEADS, 2, PEER_HALF)
        s1 = jnp.einsum('thd,kd->thk', q[:, :, 0], k1).astype(jnp.float32)
        s2 = jnp.einsum('thd,kd->thk', q[:, :, 1], k2).astype(jnp.float32)
        v1, i1 = lax.top_k(s1, PEER_TOPK)
        v2, i2 = lax.top_k(s2, PEER_TOPK)
        cand = (v1[..., :, None] + v2[..., None, :]).reshape(T, PEER_HEADS, PEER_TOPK * PEER_TOPK)
        sc, ci = lax.top_k(cand, PEER_TOPK)
        e = (jnp.take_along_axis(i1, ci // PEER_TOPK, axis=-1) * PEER_N_KEYS
             + jnp.take_along_axis(i2, ci % PEER_TOPK, axis=-1))
        g = jax.nn.softmax(sc, axis=-1)
        act = jax.nn.gelu(jnp.einsum('thkd,td->thk', jnp.take(u, e, axis=0), xb).astype(jnp.float32))
        return jnp.einsum('thk,thkd->td', (g * act).astype(xb.dtype), jnp.take(v, e, axis=0))

    return lax.map(block, xt).reshape(B, S, D)


def setup_inputs(seed: int = 0) -> dict:
    key = jax.random.key(seed)
    ks = jax.random.split(key, 20)
    f32 = jnp.float32
    nrm = lambda k, shape, scale: jax.random.normal(k, shape, f32) * scale
    col_scale = np.concatenate([np.ones(SB_WIDTH), np.ones(SB_WIDTH), np.full(SB_WIDTH, BETA),
                                np.ones(DF_QK_WIDTH), np.ones(DF_QK_WIDTH), np.full(DF_WIDTH, BETA),
                                np.ones(2 * D_MODEL)]).astype(np.float32)
    return {
        "x": nrm(ks[0], (BATCH, SEQ, D_MODEL), 1.0),
        "w_in": nrm(ks[1], (DEPTH, D_MODEL, IN_COLS), D_MODEL ** -0.5) * jnp.asarray(col_scale),
        "lambda_q1": nrm(ks[2], (DEPTH, DF_HEAD_DIM), 0.1),
        "lambda_k1": nrm(ks[3], (DEPTH, DF_HEAD_DIM), 0.1),
        "lambda_q2": nrm(ks[4], (DEPTH, DF_HEAD_DIM), 0.1),
        "lambda_k2": nrm(ks[5], (DEPTH, DF_HEAD_DIM), 0.1),
        "diff_subln_g": 1.0 + nrm(ks[6], (DEPTH, DF_V_DIM), 0.02),
        "rel_bias": nrm(ks[7], (N_BUCKETS, DF_HEADS), 0.5),
        "w_branch_a": nrm(ks[8], (DEPTH, SB_WIDTH, D_MODEL), SB_WIDTH ** -0.5 * BETA),
        "w_branch_b": nrm(ks[9], (DEPTH, DF_WIDTH, D_MODEL), DF_WIDTH ** -0.5 * BETA),
        "w_out": nrm(ks[10], (DEPTH, D_MODEL, D_MODEL), D_MODEL ** -0.5 * BETA),
        "ln1_g": 1.0 + nrm(ks[11], (DEPTH, D_MODEL), 0.02),
        "ln1_b": nrm(ks[12], (DEPTH, D_MODEL), 0.02),
        "peer_wq": nrm(ks[13], (DEPTH, D_MODEL, PEER_HEADS * PEER_QUERY_DIM), D_MODEL ** -0.5),
        "peer_k1": nrm(ks[14], (DEPTH, PEER_N_KEYS, PEER_HALF), PEER_HALF ** -0.5),
        "peer_k2": nrm(ks[15], (DEPTH, PEER_N_KEYS, PEER_HALF), PEER_HALF ** -0.5),
        "peer_u": nrm(ks[16], (DEPTH, PEER_N_EXPERTS, D_MODEL), D_MODEL ** -0.5),
        "peer_v": nrm(ks[17], (DEPTH, PEER_N_EXPERTS, D_MODEL), BETA),
        "ln2_g": 1.0 + nrm(ks[18], (DEPTH, D_MODEL), 0.02),
        "ln2_b": nrm(ks[19], (DEPTH, D_MODEL), 0.02),
    }


def reference(x, w_in, lambda_q1, lambda_k1, lambda_q2, lambda_k2, diff_subln_g, rel_bias,
              w_branch_a, w_branch_b, w_out, ln1_g, ln1_b, peer_wq, peer_k1, peer_k2,
              peer_u, peer_v, ln2_g, ln2_b):
    h = x
    for l in range(DEPTH):
        mix = token_mixer(h, w_in[l], lambda_q1[l], lambda_k1[l], lambda_q2[l], lambda_k2[l],
                          diff_subln_g[l], rel_bias, w_branch_a[l], w_branch_b[l], w_out[l], l)
        h = layer_norm(ALPHA * h + mix, ln1_g[l], ln1_b[l])
        ffn = peer_ffn(h, peer_wq[l], peer_k1[l], peer_k2[l], peer_u[l], peer_v[l])
        h = layer_norm(ALPHA * h + ffn, ln2_g[l], ln2_b[l])
    return h
```

```python
import functools
import math

import jax
import jax.numpy as jnp
from jax import lax
from jax.experimental import pallas as pl
from jax.experimental.pallas import tpu as pltpu

F32 = jnp.float32
BF16 = jnp.bfloat16

CHUNK = 64
SB_HEADS = 8
SB_HEAD_DIM = 64
DF_HEADS = 4
DF_HEAD_DIM = 64
N_BUCKETS = 32
MAX_DISTANCE = 256
PEER_HEADS = 8
PEER_N_KEYS = 128
PEER_TOPK = 16
LN_EPS = 1e-5

LANES = 128
SUBLANES = 8
HALF = SUBLANES // 2
NEG_BIG = -1e30
EXP_ZERO_BELOW = -104.0

NT_DIMS = (((1,), (1,)), ((), ()))


def _vmem_limit(mib):
    return pltpu.CompilerParams(vmem_limit_bytes=mib * 1024 * 1024)


def _inproj_kernel(x_ref, w_ref, o_ref):
    o_ref[...] = jnp.dot(x_ref[...].astype(BF16), w_ref[...],
                         preferred_element_type=F32).astype(o_ref.dtype)


def _inproj(x2, w_bf16, tm=512, tn=1024):
    T, K = x2.shape
    N = w_bf16.shape[1]
    return pl.pallas_call(
        _inproj_kernel,
        out_shape=jax.ShapeDtypeStruct((T, N), BF16),
        grid=(T // tm, N // tn),
        in_specs=[pl.BlockSpec((tm, K), lambda i, j: (i, 0)),
                  pl.BlockSpec((K, tn), lambda i, j: (0, j))],
        out_specs=pl.BlockSpec((tm, tn), lambda i, j: (i, j)),
        compiler_params=_vmem_limit(40),
        name="in_proj",
    )(x2, w_bf16)


def _sb_kernel(q_ref, k_ref, v_ref, m_ref, o_ref, acc_ref, r_ref, *, tq):
    i = pl.program_id(2)
    q = q_ref[...]
    lane = lax.broadcasted_iota(jnp.int32, (tq, LANES), 1)
    zero = jnp.zeros_like(q)
    q_heads = (jnp.where(lane < SB_HEAD_DIM, q, zero), jnp.where(lane >= SB_HEAD_DIM, q, zero))
    row = lax.broadcasted_iota(jnp.int32, (tq, tq), 0)
    col = lax.broadcasted_iota(jnp.int32, (tq, tq), 1)
    strict = col < row

    acc_ref[...] = jnp.zeros_like(acc_ref)
    r_ref[...] = jnp.zeros_like(r_ref)

    def visit(j, diag):
        start = pl.multiple_of(j * tq, tq)
        kb = k_ref[pl.ds(start, tq), :]
        vb = v_ref[pl.ds(start, tq), :]
        mext = m_ref[...]
        for h in range(2):
            z = lax.dot_general(q_heads[h], kb, NT_DIMS, preferred_element_type=F32)
            log_beta = jnp.minimum(z, 0.0) - jnp.log(1.0 + jnp.exp(-jnp.abs(z)))
            log_fail = log_beta - z
            if diag:
                log_fail = jnp.where(strict, log_fail, 0.0)
            hi = log_fail.astype(BF16)
            lo = (log_fail - hi.astype(F32)).astype(BF16)
            cum = (jnp.dot(hi, mext, preferred_element_type=F32)
                   + jnp.dot(lo, mext, preferred_element_type=F32))
            r_old = r_ref[h]
            arg = log_beta + cum[:, :tq] + jnp.tile(r_old, (1, tq // LANES))
            w = jnp.exp(arg)
            if diag:
                w = jnp.where(strict, w, 0.0)
            acc_ref[h] += jnp.dot(w.astype(BF16), vb, preferred_element_type=F32)
            r_ref[h] = r_old + cum[:, tq:]

    visit(i, True)

    def cond(c):
        j, live = c
        return jnp.logical_and(j >= 0, live > 0)

    def body(c):
        j, _ = c
        visit(j, False)
        r_max = jnp.max(jnp.maximum(r_ref[0], r_ref[1]))
        return j - 1, (r_max >= EXP_ZERO_BELOW).astype(jnp.int32)

    lax.while_loop(cond, body, (i - 1, jnp.int32(1)))
    o_ref[...] = jnp.where(lane < SB_HEAD_DIM, acc_ref[0], acc_ref[1]).astype(o_ref.dtype)


def _sb_attention(proj, mext, B, S, tq):
    n_pairs = SB_HEADS * SB_HEAD_DIM // LANES
    nq = S // tq
    return pl.pallas_call(
        functools.partial(_sb_kernel, tq=tq),
        out_shape=jax.ShapeDtypeStruct((B * S, n_pairs * LANES), BF16),
        grid=(B, n_pairs, nq),
        in_specs=[
            pl.BlockSpec((tq, LANES), lambda b, p, i: (b * nq + i, p)),
            pl.BlockSpec((S, LANES), lambda b, p, i: (b, n_pairs + p)),
            pl.BlockSpec((S, LANES), lambda b, p, i: (b, 2 * n_pairs + p)),
            pl.BlockSpec((tq, tq + LANES), lambda b, p, i: (0, 0)),
        ],
        out_specs=pl.BlockSpec((tq, LANES), lambda b, p, i: (b * nq + i, p)),
        scratch_shapes=[pltpu.VMEM((2, tq, LANES), F32), pltpu.VMEM((2, tq, LANES), F32)],
        compiler_params=_vmem_limit(40),
        name="sb_attn",
    )(proj, proj, proj, mext)


def _df_kernel(lam_ref, q_ref, k_ref, v_ref, bias_ref, g_ref, o_ref, acc_ref, m_ref, l_ref,
               *, tq, n_near, lam_init):
    i = pl.program_id(2)
    q = q_ref[...]
    lane = lax.broadcasted_iota(jnp.int32, (tq, LANES), 1)
    zero = jnp.zeros_like(q)
    q_maps = (jnp.where(lane < DF_HEAD_DIM, q, zero), jnp.where(lane >= DF_HEAD_DIM, q, zero))

    acc_ref[...] = jnp.zeros_like(acc_ref)
    l_ref[...] = jnp.zeros_like(l_ref)
    m_ref[...] = jnp.full_like(m_ref, NEG_BIG)

    def body(j, carry):
        start = pl.multiple_of(j * tq, tq)
        kb = k_ref[pl.ds(start, tq), :]
        vb = v_ref[pl.ds(start, tq), :]
        bias = bias_ref[0, jnp.minimum(i - j, n_near)]
        for m in range(2):
            s = lax.dot_general(q_maps[m], kb, NT_DIMS, preferred_element_type=F32) + bias
            m_old = m_ref[m]
            m_new = jnp.maximum(m_old, jnp.max(s, axis=-1, keepdims=True))
            p = jnp.exp(s - jnp.tile(m_new, (1, tq // LANES)))
            corr = jnp.exp(m_old - m_new)
            l_ref[m] = corr * l_ref[m] + jnp.sum(p, axis=-1, keepdims=True)
            acc_ref[m] = corr * acc_ref[m] + jnp.dot(p.astype(BF16), vb, preferred_element_type=F32)
            m_ref[m] = m_new
        return carry

    lax.fori_loop(0, i + 1, body, 0)

    lam_vec = lam_ref[...]
    lam = (jnp.exp(jnp.sum(lam_vec[0:1] * lam_vec[1:2], axis=-1, keepdims=True))
           - jnp.exp(jnp.sum(lam_vec[2:3] * lam_vec[3:4], axis=-1, keepdims=True)) + lam_init)
    o = acc_ref[0] / l_ref[0] - lam * (acc_ref[1] / l_ref[1])
    o = o * lax.rsqrt(jnp.mean(o * o, axis=-1, keepdims=True) + LN_EPS)
    o_ref[...] = (o * g_ref[...] * (1.0 - lam_init)).astype(o_ref.dtype)


def _df_attention(proj, lam_vec, bias_tiles, subln_g, B, S, tq, lam_init):
    nq = S // tq
    n_near = bias_tiles.shape[1] - 1
    q0 = 3 * SB_HEADS * SB_HEAD_DIM // LANES
    return pl.pallas_call(
        functools.partial(_df_kernel, tq=tq, n_near=n_near, lam_init=lam_init),
        out_shape=jax.ShapeDtypeStruct((B * S, DF_HEADS * LANES), BF16),
        grid=(B, DF_HEADS, nq),
        in_specs=[
            pl.BlockSpec((4, DF_HEAD_DIM), lambda b, h, i: (0, 0)),
            pl.BlockSpec((tq, LANES), lambda b, h, i: (b * nq + i, q0 + h)),
            pl.BlockSpec((S, LANES), lambda b, h, i: (b, q0 + DF_HEADS + h)),
            pl.BlockSpec((S, LANES), lambda b, h, i: (b, q0 + 2 * DF_HEADS + h)),
            pl.BlockSpec((1, n_near + 1, tq, tq), lambda b, h, i: (h, 0, 0, 0)),
            pl.BlockSpec((1, LANES), lambda b, h, i: (0, 0)),
        ],
        out_specs=pl.BlockSpec((tq, LANES), lambda b, h, i: (b * nq + i, h)),
        scratch_shapes=[pltpu.VMEM((2, tq, LANES), F32), pltpu.VMEM((2, tq, LANES), F32),
                        pltpu.VMEM((2, tq, LANES), F32)],
        compiler_params=_vmem_limit(40),
        name="df_attn",
    )(lam_vec, proj, proj, proj, bias_tiles, subln_g)


def _t5_bucket(rel):
    half = N_BUCKETS // 2
    max_exact = half // 2
    ret = (rel > 0).astype(jnp.int32) * half
    n = jnp.abs(rel)
    nf = jnp.maximum(n, 1).astype(F32)
    large = max_exact + (jnp.log(nf / max_exact) / math.log(MAX_DISTANCE / max_exact)
                         * (half - max_exact)).astype(jnp.int32)
    large = jnp.minimum(large, half - 1)
    return ret + jnp.where(n < max_exact, n, large)


def _df_bias_tiles(rel_bias, tq):
    n_near = -(-(MAX_DISTANCE - 1) // tq) + 1
    r = jnp.arange(tq, dtype=jnp.int32)[:, None]
    c = jnp.arange(tq, dtype=jnp.int32)[None, :]
    tiles = []
    for d in range(n_near + 1):
        b = rel_bias[_t5_bucket(c - r - d * tq)].astype(F32)
        if d == 0:
            allowed = (c // CHUNK) <= (r // CHUNK)
            b = jnp.where(allowed[:, :, None], b, NEG_BIG)
        tiles.append(b.transpose(2, 0, 1))
    return jnp.stack(tiles, axis=1)


def _layer_norm(y, g, b):
    mu = jnp.mean(y, axis=-1, keepdims=True)
    d = y - mu
    var = jnp.mean(d * d, axis=-1, keepdims=True)
    return d * lax.rsqrt(var + LN_EPS) * g + b


def _merge_kernel(osb_ref, odf_ref, ga_ref, gb_ref, x_ref, wa_ref, wb_ref, wo_ref, g_ref, b_ref,
                  h_ref, *, alpha):
    a = jnp.dot(osb_ref[...], wa_ref[...], preferred_element_type=F32)
    b = jnp.dot(odf_ref[...], wb_ref[...], preferred_element_type=F32)
    merged = (jax.nn.sigmoid(ga_ref[...].astype(F32)) * a
              + jax.nn.sigmoid(gb_ref[...].astype(F32)) * b)
    mix = jnp.dot(merged.astype(BF16), wo_ref[...], preferred_element_type=F32)
    h_ref[...] = _layer_norm(alpha * x_ref[...] + mix, g_ref[...], b_ref[...])


def _merge(o_sb, o_df, proj, x2, w_a, w_b, w_o, ln_g, ln_b, alpha, tm=512):
    T, D = x2.shape
    Wa, Wb = o_sb.shape[1], o_df.shape[1]
    gate_blk = (proj.shape[1] - 2 * D) // D
    const = lambda i: (0, 0)
    return pl.pallas_call(
        functools.partial(_merge_kernel, alpha=alpha),
        out_shape=jax.ShapeDtypeStruct((T, D), F32),
        grid=(T // tm,),
        in_specs=[
            pl.BlockSpec((tm, Wa), lambda i: (i, 0)),
            pl.BlockSpec((tm, Wb), lambda i: (i, 0)),
            pl.BlockSpec((tm, D), lambda i: (i, gate_blk)),
            pl.BlockSpec((tm, D), lambda i: (i, gate_blk + 1)),
            pl.BlockSpec((tm, D), lambda i: (i, 0)),
            pl.BlockSpec((Wa, D), const),
            pl.BlockSpec((Wb, D), const),
            pl.BlockSpec((D, D), const),
            pl.BlockSpec((1, D), const),
            pl.BlockSpec((1, D), const),
        ],
        out_specs=pl.BlockSpec((tm, D), lambda i: (i, 0)),
        compiler_params=_vmem_limit(40),
        name="merge_ln1",
    )(o_sb, o_df, proj, proj, x2, w_a, w_b, w_o, ln_g, ln_b)


def _top16(s, n_rows, tm):
    rows = lax.broadcasted_iota(jnp.int32, (n_rows, tm), 0)
    vals, idxs = [], []
    for _ in range(PEER_TOPK):
        m = jnp.max(s, axis=0, keepdims=True)
        idx = jnp.min(jnp.where(s == m, rows, n_rows), axis=0, keepdims=True)
        vals.append(m)
        idxs.append(idx)
        s = jnp.where(rows == idx, -jnp.inf, s)
    return jnp.concatenate(vals, axis=0), jnp.concatenate(idxs, axis=0)


def _pick_rows(table, sel):
    out = jnp.zeros_like(table)
    for a in range(PEER_TOPK):
        out = jnp.where(sel == a, table[a:a + 1, :], out)
    return out


def _peer_select_kernel(h_ref, wq_ref, k1_ref, k2_ref, e_ref, g_ref, *, tm, row_words):
    q = jnp.dot(h_ref[...].astype(BF16), wq_ref[...], preferred_element_type=F32)
    k1 = k1_ref[...]
    k2 = k2_ref[...]
    half = k1.shape[1]
    for hd in range(PEER_HEADS):
        q1 = q[:, (2 * hd) * half:(2 * hd + 1) * half].astype(BF16)
        q2 = q[:, (2 * hd + 1) * half:(2 * hd + 2) * half].astype(BF16)
        s1 = lax.dot_general(k1, q1, NT_DIMS, preferred_element_type=F32)
        s2 = lax.dot_general(k2, q2, NT_DIMS, preferred_element_type=F32)
        v1, i1 = _top16(s1, PEER_N_KEYS, tm)
        v2, i2 = _top16(s2, PEER_N_KEYS, tm)
        cand = jnp.concatenate([v1[a:a + 1, :] + v2 for a in range(PEER_TOPK)], axis=0)
        sc, ci = _top16(cand, PEER_TOPK * PEER_TOPK, tm)
        e = (_pick_rows(i1, ci // PEER_TOPK) * PEER_N_KEYS + _pick_rows(i2, ci % PEER_TOPK))
        p = jnp.exp(sc - sc[0:1, :])
        g = p / jnp.sum(p, axis=0, keepdims=True)
        e_ref[hd * PEER_TOPK:(hd + 1) * PEER_TOPK, :] = e * row_words + SUBLANES
        g_ref[hd * PEER_TOPK:(hd + 1) * PEER_TOPK, :] = g


def _peer_select(h2, wq, k1, k2, row_words, tm=256):
    T, D = h2.shape
    n_slots = PEER_HEADS * PEER_TOPK
    const = lambda i: (0, 0)
    return pl.pallas_call(
        functools.partial(_peer_select_kernel, tm=tm, row_words=row_words),
        out_shape=(jax.ShapeDtypeStruct((n_slots, T), jnp.int32),
                   jax.ShapeDtypeStruct((n_slots, T), F32)),
        grid=(T // tm,),
        in_specs=[pl.BlockSpec((tm, D), lambda i: (i, 0)),
                  pl.BlockSpec(wq.shape, const),
                  pl.BlockSpec(k1.shape, const),
                  pl.BlockSpec(k2.shape, const)],
        out_specs=(pl.BlockSpec((n_slots, tm), lambda i: (0, i)),
                   pl.BlockSpec((n_slots, tm), lambda i: (0, i))),
        compiler_params=_vmem_limit(40),
        name="peer_select",
    )(h2, wq, k1, k2)


def _load_table(tbl_hbm, tbl, sem):
    n_words = tbl_hbm.shape[0]

    @pl.when(pl.program_id(0) == 0)
    def _():
        pad = jnp.zeros((SUBLANES, LANES), tbl.dtype)
        tbl[pl.ds(0, SUBLANES), :] = pad
        tbl[pl.ds(SUBLANES + n_words, SUBLANES), :] = pad
        cp = pltpu.make_async_copy(tbl_hbm, tbl.at[pl.ds(SUBLANES, n_words)], sem)
        cp.start()
        cp.wait()


def _pair_rows(tbl, e_ref, p, t, lo_half):
    n_pairs = e_ref.shape[0] // 2
    va = tbl[pl.ds(e_ref[p, t], SUBLANES), :]
    vb = tbl[pl.ds(e_ref[p + n_pairs, t] - HALF, SUBLANES), :]
    pr = jnp.where(lo_half, va, vb)
    lo = lax.bitcast_convert_type(pr << 16, F32)
    hi = lax.bitcast_convert_type(pr & jnp.uint32(0xFFFF0000), F32)
    return lo, hi


def _peer_u_kernel(e_ref, x_ref, g_ref, tbl_hbm, w_ref, tbl, sem, *, tb):
    _load_table(tbl_hbm, tbl, sem)
    n_pairs = e_ref.shape[0] // 2
    sub = lax.broadcasted_iota(jnp.int32, (SUBLANES, LANES), 0)
    lane = lax.broadcasted_iota(jnp.int32, (SUBLANES, LANES), 1)
    lo_half = sub < HALF

    def token(t, carry):
        xt = x_ref[t]
        xr = pltpu.roll(xt, HALF, axis=0)
        x_lo = jnp.where(lo_half, xt, xr)
        x_hi = jnp.where(lo_half, xr, xt)
        tile = jnp.zeros((SUBLANES, LANES), F32)
        for p in range(n_pairs):
            lo, hi = _pair_rows(tbl, e_ref, p, t, lo_half)
            part = jnp.sum(lo * x_lo + hi * x_hi, axis=-1, keepdims=True)
            tile = jnp.where(lane == p, part, tile)
        tile = tile + pltpu.roll(tile, SUBLANES - 1, axis=0)
        tile = tile + pltpu.roll(tile, SUBLANES - 2, axis=0)
        act = jnp.where(lane[0:1] < n_pairs, tile[0:1], pltpu.roll(tile, n_pairs, axis=1)[HALF:HALF + 1])
        w_ref[pl.ds(t, 1), :] = g_ref[pl.ds(t, 1), :] * jax.nn.gelu(act)
        return carry

    lax.fori_loop(0, tb, token, 0)


def _peer_v_kernel(e_ref, w_ref, h_ref, tbl_hbm, g_ref, b_ref, o_ref, tbl, sem, *, tb, alpha):
    _load_table(tbl_hbm, tbl, sem)
    n_pairs = e_ref.shape[0] // 2
    sub = lax.broadcasted_iota(jnp.int32, (SUBLANES, LANES), 0)
    lo_half = sub < HALF
    n_feat = SUBLANES * LANES

    def token(t, carry):
        acc = [jnp.zeros((SUBLANES, LANES), F32) for _ in range(4)]
        for p in range(n_pairs):
            lo, hi = _pair_rows(tbl, e_ref, p, t, lo_half)
            wp = jnp.where(lo_half, w_ref[t, p], w_ref[t, p + n_pairs])
            k = 2 * (p % 2)
            acc[k] = acc[k] + lo * wp
            acc[k + 1] = acc[k + 1] + hi * wp
        a_lo = acc[0] + acc[2]
        a_hi = acc[1] + acc[3]
        ffn = jnp.where(lo_half, a_lo + pltpu.roll(a_lo, HALF, axis=0), a_hi + pltpu.roll(a_hi, HALF, axis=0))
        y = alpha * h_ref[t] + ffn
        mu = jnp.sum(jnp.sum(y, axis=-1, keepdims=True), axis=0, keepdims=True) / n_feat
        d = y - mu
        var = jnp.sum(jnp.sum(d * d, axis=-1, keepdims=True), axis=0, keepdims=True) / n_feat
        o_ref[t] = d * lax.rsqrt(var + LN_EPS) * g_ref[...] + b_ref[...]
        return carry

    lax.fori_loop(0, tb, token, 0)


def _pack_table(tab):
    E, D = tab.shape
    bits = lax.bitcast_convert_type(tab.astype(BF16), jnp.uint16).astype(jnp.uint32)
    packed = bits[:, :D // 2] | (bits[:, D // 2:] << 16)
    return packed.reshape(E * (D // 2) // LANES, LANES)


def _peer_scratch(tbl_words):
    return [pltpu.VMEM((tbl_words.shape[0] + 2 * SUBLANES, LANES), jnp.uint32), pltpu.SemaphoreType.DMA]


def _peer_u(e_t, h_tiles, g_tok, tbl_words, tb=128):
    n_slots, T = e_t.shape
    return pl.pallas_call(
        functools.partial(_peer_u_kernel, tb=tb),
        out_shape=jax.ShapeDtypeStruct((T, n_slots), F32),
        grid=(T // tb,),
        in_specs=[
            pl.BlockSpec((n_slots, tb), lambda i: (0, i), memory_space=pltpu.SMEM),
            pl.BlockSpec((tb, SUBLANES, LANES), lambda i: (i, 0, 0)),
            pl.BlockSpec((tb, n_slots), lambda i: (i, 0)),
            pl.BlockSpec(memory_space=pl.ANY),
        ],
        out_specs=pl.BlockSpec((tb, n_slots), lambda i: (i, 0)),
        scratch_shapes=_peer_scratch(tbl_words),
        compiler_params=_vmem_limit(48),
        name="peer_u",
    )(e_t, h_tiles, g_tok, tbl_words)


def _peer_v(e_t, w_tok, h_tiles, tbl_words, ln_g, ln_b, alpha, tb=128):
    n_slots, T = e_t.shape
    const = lambda i: (0, 0)
    return pl.pallas_call(
        functools.partial(_peer_v_kernel, tb=tb, alpha=alpha),
        out_shape=jax.ShapeDtypeStruct((T, SUBLANES, LANES), F32),
        grid=(T // tb,),
        in_specs=[
            pl.BlockSpec((n_slots, tb), lambda i: (0, i), memory_space=pltpu.SMEM),
            pl.BlockSpec((tb, n_slots), lambda i: (i, 0), memory_space=pltpu.SMEM),
            pl.BlockSpec((tb, SUBLANES, LANES), lambda i: (i, 0, 0)),
            pl.BlockSpec(memory_space=pl.ANY),
            pl.BlockSpec((SUBLANES, LANES), const),
            pl.BlockSpec((SUBLANES, LANES), const),
        ],
        out_specs=pl.BlockSpec((tb, SUBLANES, LANES), lambda i: (i, 0, 0)),
        scratch_shapes=_peer_scratch(tbl_words),
        compiler_params=_vmem_limit(48),
        name="peer_v",
    )(e_t, w_tok, h_tiles, tbl_words, ln_g, ln_b)


def kernel(x, w_in, lambda_q1, lambda_k1, lambda_q2, lambda_k2, diff_subln_g, rel_bias, w_branch_a, w_branch_b, w_out, ln1_g, ln1_b, peer_wq, peer_k1, peer_k2, peer_u, peer_v, ln2_g, ln2_b):
    B, S, D = x.shape
    T = B * S
    depth = w_in.shape[0]
    alpha = (2.0 * depth) ** 0.25
    tq = 256
    assert D == SUBLANES * LANES and S % tq == 0 and tq % CHUNK == 0

    sb_w = SB_HEADS * SB_HEAD_DIM
    df_w = DF_HEADS * 2 * DF_HEAD_DIM
    assert SB_HEAD_DIM == 64 and DF_HEAD_DIM == 64
    col_scale = jnp.ones((w_in.shape[2],), F32)
    col_scale = col_scale.at[:sb_w].set(0.125).at[3 * sb_w:3 * sb_w + df_w].set(0.125)

    r = jnp.arange(tq, dtype=jnp.int32)
    later = (r[:, None] > r[None, :]).astype(BF16)
    mext = jnp.concatenate([later, jnp.ones((tq, LANES), BF16)], axis=1)
    bias_tiles = _df_bias_tiles(rel_bias, tq)
    row_words = D // 2 // LANES

    h = x.reshape(T, D)
    for l in range(depth):
        lam_init = 0.8 - 0.6 * math.exp(-0.3 * l)
        proj = _inproj(h, (w_in[l] * col_scale).astype(BF16))
        o_sb = _sb_attention(proj, mext, B, S, tq)
        lam_vec = jnp.stack([lambda_q1[l], lambda_k1[l], lambda_q2[l], lambda_k2[l]]).astype(F32)
        o_df = _df_attention(proj, lam_vec, bias_tiles, diff_subln_g[l].reshape(1, -1).astype(F32),
                             B, S, tq, lam_init)
        h = _merge(o_sb, o_df, proj, h, w_branch_a[l].astype(BF16), w_branch_b[l].astype(BF16),
                   w_out[l].astype(BF16), ln1_g[l].reshape(1, D), ln1_b[l].reshape(1, D), alpha)
        e_t, g_t = _peer_select(h, peer_wq[l].astype(BF16), peer_k1[l].astype(BF16),
                                peer_k2[l].astype(BF16), row_words)
        h_tiles = h.reshape(T, SUBLANES, LANES)
        w_tok = _peer_u(e_t, h_tiles, g_t.T, _pack_table(peer_u[l]))
        h = _peer_v(e_t, w_tok, h_tiles, _pack_table(peer_v[l]),
                    ln2_g[l].reshape(SUBLANES, LANES), ln2_b[l].reshape(SUBLANES, LANES), alpha).reshape(T, D)
    return h.reshape(B, S, D)
```

```python
import functools
import math

import jax
import jax.numpy as jnp
from jax import lax
from jax.experimental import pallas as pl
from jax.experimental.pallas import tpu as pltpu

F32 = jnp.float32
BF16 = jnp.bfloat16

CHUNK = 64
SB_HEADS = 8
SB_HEAD_DIM = 64
DF_HEADS = 4
DF_HEAD_DIM = 64
N_BUCKETS = 32
MAX_DISTANCE = 256
PEER_HEADS = 8
PEER_N_KEYS = 128
PEER_TOPK = 16
LN_EPS = 1e-5

LANES = 128
SUBLANES = 8
HALF = SUBLANES // 2
TOKENS_PER_ITER = 4
NEG_BIG = -1e30
EXP_ZERO_BELOW = -104.0

NT_DIMS = (((1,), (1,)), ((), ()))


def _vmem_limit(mib):
    return pltpu.CompilerParams(vmem_limit_bytes=mib * 1024 * 1024)


def _inproj_kernel(x_ref, w_ref, o_ref):
    o_ref[...] = jnp.dot(x_ref[...].astype(BF16), w_ref[...],
                         preferred_element_type=F32).astype(o_ref.dtype)


def _inproj(x2, w_bf16, tm=512, tn=1024):
    T, K = x2.shape
    N = w_bf16.shape[1]
    return pl.pallas_call(
        _inproj_kernel,
        out_shape=jax.ShapeDtypeStruct((T, N), BF16),
        grid=(T // tm, N // tn),
        in_specs=[pl.BlockSpec((tm, K), lambda i, j: (i, 0)),
                  pl.BlockSpec((K, tn), lambda i, j: (0, j))],
        out_specs=pl.BlockSpec((tm, tn), lambda i, j: (i, j)),
        compiler_params=_vmem_limit(40),
        name="in_proj",
    )(x2, w_bf16)


def _sb_kernel(q_ref, k_ref, v_ref, m_ref, o_ref, acc_ref, r_ref, *, tq):
    i = pl.program_id(2)
    q = q_ref[...]
    lane = lax.broadcasted_iota(jnp.int32, (tq, LANES), 1)
    zero = jnp.zeros_like(q)
    q_heads = (jnp.where(lane < SB_HEAD_DIM, q, zero), jnp.where(lane >= SB_HEAD_DIM, q, zero))
    row = lax.broadcasted_iota(jnp.int32, (tq, tq), 0)
    col = lax.broadcasted_iota(jnp.int32, (tq, tq), 1)
    strict = col < row

    acc_ref[...] = jnp.zeros_like(acc_ref)
    r_ref[...] = jnp.zeros_like(r_ref)

    def visit(j, diag):
        start = pl.multiple_of(j * tq, tq)
        kb = k_ref[pl.ds(start, tq), :]
        vb = v_ref[pl.ds(start, tq), :]
        mext = m_ref[...]
        for h in range(2):
            z = lax.dot_general(q_heads[h], kb, NT_DIMS, preferred_element_type=F32)
            log_beta = jnp.minimum(z, 0.0) - jnp.log(1.0 + jnp.exp(-jnp.abs(z)))
            log_fail = log_beta - z
            if diag:
                log_fail = jnp.where(strict, log_fail, 0.0)
            hi = log_fail.astype(BF16)
            lo = (log_fail - hi.astype(F32)).astype(BF16)
            cum = (jnp.dot(hi, mext, preferred_element_type=F32)
                   + jnp.dot(lo, mext, preferred_element_type=F32))
            r_old = r_ref[h]
            arg = log_beta + cum[:, :tq] + jnp.tile(r_old, (1, tq // LANES))
            w = jnp.exp(arg)
            if diag:
                w = jnp.where(strict, w, 0.0)
            acc_ref[h] += jnp.dot(w.astype(BF16), vb, preferred_element_type=F32)
            r_ref[h] = r_old + cum[:, tq:]

    visit(i, True)

    def cond(c):
        j, live = c
        return jnp.logical_and(j >= 0, live > 0)

    def body(c):
        j, _ = c
        visit(j, False)
        r_max = jnp.max(jnp.maximum(r_ref[0], r_ref[1]))
        return j - 1, (r_max >= EXP_ZERO_BELOW).astype(jnp.int32)

    lax.while_loop(cond, body, (i - 1, jnp.int32(1)))
    o_ref[...] = jnp.where(lane < SB_HEAD_DIM, acc_ref[0], acc_ref[1]).astype(o_ref.dtype)


def _sb_attention(proj, mext, B, S, tq):
    n_pairs = SB_HEADS * SB_HEAD_DIM // LANES
    nq = S // tq
    return pl.pallas_call(
        functools.partial(_sb_kernel, tq=tq),
        out_shape=jax.ShapeDtypeStruct((B * S, n_pairs * LANES), BF16),
        grid=(B, n_pairs, nq),
        in_specs=[
            pl.BlockSpec((tq, LANES), lambda b, p, i: (b * nq + i, p)),
            pl.BlockSpec((S, LANES), lambda b, p, i: (b, n_pairs + p)),
            pl.BlockSpec((S, LANES), lambda b, p, i: (b, 2 * n_pairs + p)),
            pl.BlockSpec((tq, tq + LANES), lambda b, p, i: (0, 0)),
        ],
        out_specs=pl.BlockSpec((tq, LANES), lambda b, p, i: (b * nq + i, p)),
        scratch_shapes=[pltpu.VMEM((2, tq, LANES), F32), pltpu.VMEM((2, tq, LANES), F32)],
        compiler_params=_vmem_limit(40),
        name="sb_attn",
    )(proj, proj, proj, mext)


def _df_kernel(far_ref, lam_ref, q_ref, k_ref, v_ref, bias_ref, g_ref, o_ref, acc_ref, m_ref,
               *, tq, n_near, lam_init):
    i = pl.program_id(2)
    q = q_ref[...]
    lane = lax.broadcasted_iota(jnp.int32, (tq, LANES), 1)
    zero = jnp.zeros_like(q)
    q_maps = (jnp.where(lane < DF_HEAD_DIM, q, zero), jnp.where(lane >= DF_HEAD_DIM, q, zero))

    acc_ref[...] = jnp.zeros_like(acc_ref)
    m_ref[...] = jnp.full_like(m_ref, NEG_BIG)

    def visit(start, width, bias, shift):
        kb = k_ref[pl.ds(start, width), :]
        vb = v_ref[pl.ds(start, width), :]
        v_ext = jnp.concatenate([vb, jnp.ones((width, LANES), BF16)], axis=1)
        for m in range(2):
            s = lax.dot_general(q_maps[m], kb, NT_DIMS, preferred_element_type=F32)
            if bias is not None:
                s = s + bias
            m_old = m_ref[m]
            m_new = jnp.maximum(m_old, jnp.max(s, axis=-1, keepdims=True) + shift)
            p = jnp.exp(s - jnp.tile(m_new - shift, (1, width // LANES)))
            corr = jnp.exp(m_old - m_new)
            acc_ref[m] = (jnp.tile(corr, (1, 2)) * acc_ref[m]
                          + jnp.dot(p.astype(BF16), v_ext, preferred_element_type=F32))
            m_ref[m] = m_new

    far_bias = far_ref[pl.program_id(1)]
    n_far = jnp.maximum(i - n_near + 1, 0)

    def far_pair(jj, carry):
        visit(pl.multiple_of(jj * 2 * tq, 2 * tq), 2 * tq, None, far_bias)
        return carry

    lax.fori_loop(0, n_far // 2, far_pair, 0)

    @pl.when(n_far % 2 == 1)
    def _():
        visit(pl.multiple_of((n_far - 1) * tq, tq), tq, None, far_bias)

    for d in range(n_near - 1, -1, -1):
        @pl.when(i >= d)
        def _():
            visit(pl.multiple_of((i - d) * tq, tq), tq, bias_ref[0, d], 0.0)

    lam_vec = lam_ref[...]
    lam = (jnp.exp(jnp.sum(lam_vec[0:1] * lam_vec[1:2], axis=-1, keepdims=True))
           - jnp.exp(jnp.sum(lam_vec[2:3] * lam_vec[3:4], axis=-1, keepdims=True)) + lam_init)
    o = (acc_ref[0, :, :LANES] / acc_ref[0, :, LANES:]
         - lam * (acc_ref[1, :, :LANES] / acc_ref[1, :, LANES:]))
    o = o * lax.rsqrt(jnp.mean(o * o, axis=-1, keepdims=True) + LN_EPS)
    o_ref[...] = (o * g_ref[...] * (1.0 - lam_init)).astype(o_ref.dtype)


def _df_attention(proj, lam_vec, bias_tiles, far_bias, subln_g, B, S, tq, lam_init):
    nq = S // tq
    n_near = bias_tiles.shape[1]
    q0 = 3 * SB_HEADS * SB_HEAD_DIM // LANES
    return pl.pallas_call(
        functools.partial(_df_kernel, tq=tq, n_near=n_near, lam_init=lam_init),
        out_shape=jax.ShapeDtypeStruct((B * S, DF_HEADS * LANES), BF16),
        grid=(B, DF_HEADS, nq),
        in_specs=[
            pl.BlockSpec(memory_space=pltpu.SMEM),
            pl.BlockSpec((4, DF_HEAD_DIM), lambda b, h, i: (0, 0)),
            pl.BlockSpec((tq, LANES), lambda b, h, i: (b * nq + i, q0 + h)),
            pl.BlockSpec((S, LANES), lambda b, h, i: (b, q0 + DF_HEADS + h)),
            pl.BlockSpec((S, LANES), lambda b, h, i: (b, q0 + 2 * DF_HEADS + h)),
            pl.BlockSpec((1, n_near, tq, tq), lambda b, h, i: (h, 0, 0, 0)),
            pl.BlockSpec((1, LANES), lambda b, h, i: (0, 0)),
        ],
        out_specs=pl.BlockSpec((tq, LANES), lambda b, h, i: (b * nq + i, h)),
        scratch_shapes=[pltpu.VMEM((2, tq, 2 * LANES), F32), pltpu.VMEM((2, tq, LANES), F32)],
        compiler_params=_vmem_limit(40),
        name="df_attn",
    )(far_bias, lam_vec, proj, proj, proj, bias_tiles, subln_g)


def _t5_bucket(rel):
    half = N_BUCKETS // 2
    max_exact = half // 2
    ret = (rel > 0).astype(jnp.int32) * half
    n = jnp.abs(rel)
    nf = jnp.maximum(n, 1).astype(F32)
    large = max_exact + (jnp.log(nf / max_exact) / math.log(MAX_DISTANCE / max_exact)
                         * (half - max_exact)).astype(jnp.int32)
    large = jnp.minimum(large, half - 1)
    return ret + jnp.where(n < max_exact, n, large)


def _df_bias_tiles(rel_bias, tq):
    n_near = -(-(MAX_DISTANCE - 1) // tq) + 1
    r = jnp.arange(tq, dtype=jnp.int32)[:, None]
    c = jnp.arange(tq, dtype=jnp.int32)[None, :]
    tiles = []
    for d in range(n_near):
        b = rel_bias[_t5_bucket(c - r - d * tq)].astype(F32)
        if d == 0:
            allowed = (c // CHUNK) <= (r // CHUNK)
            b = jnp.where(allowed[:, :, None], b, NEG_BIG)
        tiles.append(b.transpose(2, 0, 1))
    far = rel_bias[_t5_bucket(jnp.int32(-MAX_DISTANCE))].astype(F32)
    return jnp.stack(tiles, axis=1), far


def _layer_norm(y, g, b):
    mu = jnp.mean(y, axis=-1, keepdims=True)
    d = y - mu
    var = jnp.mean(d * d, axis=-1, keepdims=True)
    return d * lax.rsqrt(var + LN_EPS) * g + b


def _merge_kernel(osb_ref, odf_ref, ga_ref, gb_ref, x_ref, wa_ref, wb_ref, wo_ref, g_ref, b_ref,
                  h_ref, *, alpha):
    a = jnp.dot(osb_ref[...], wa_ref[...], preferred_element_type=F32)
    b = jnp.dot(odf_ref[...], wb_ref[...], preferred_element_type=F32)
    merged = (jax.nn.sigmoid(ga_ref[...].astype(F32)) * a
              + jax.nn.sigmoid(gb_ref[...].astype(F32)) * b)
    mix = jnp.dot(merged.astype(BF16), wo_ref[...], preferred_element_type=F32)
    h_ref[...] = _layer_norm(alpha * x_ref[...] + mix, g_ref[...], b_ref[...])


def _merge(o_sb, o_df, proj, x2, w_a, w_b, w_o, ln_g, ln_b, alpha, tm=512):
    T, D = x2.shape
    Wa, Wb = o_sb.shape[1], o_df.shape[1]
    gate_blk = (proj.shape[1] - 2 * D) // D
    const = lambda i: (0, 0)
    return pl.pallas_call(
        functools.partial(_merge_kernel, alpha=alpha),
        out_shape=jax.ShapeDtypeStruct((T, D), F32),
        grid=(T // tm,),
        in_specs=[
            pl.BlockSpec((tm, Wa), lambda i: (i, 0)),
            pl.BlockSpec((tm, Wb), lambda i: (i, 0)),
            pl.BlockSpec((tm, D), lambda i: (i, gate_blk)),
            pl.BlockSpec((tm, D), lambda i: (i, gate_blk + 1)),
            pl.BlockSpec((tm, D), lambda i: (i, 0)),
            pl.BlockSpec((Wa, D), const),
            pl.BlockSpec((Wb, D), const),
            pl.BlockSpec((D, D), const),
            pl.BlockSpec((1, D), const),
            pl.BlockSpec((1, D), const),
        ],
        out_specs=pl.BlockSpec((tm, D), lambda i: (i, 0)),
        compiler_params=_vmem_limit(40),
        name="merge_ln1",
    )(o_sb, o_df, proj, proj, x2, w_a, w_b, w_o, ln_g, ln_b)


def _top16(s, ids):
    id_bound = jnp.iinfo(jnp.int32).max
    vals, idxs = [], []
    for _ in range(PEER_TOPK):
        m = jnp.max(s, axis=0, keepdims=True)
        idx = jnp.min(jnp.where(s == m, ids, id_bound), axis=0, keepdims=True)
        vals.append(m)
        idxs.append(idx)
        s = jnp.where(ids == idx, -jnp.inf, s)
    return jnp.concatenate(vals, axis=0), jnp.concatenate(idxs, axis=0)


def _pair_candidates(v1, v2, tm):
    sub = lax.broadcasted_iota(jnp.int32, (SUBLANES, tm), 0)
    k = PEER_TOPK
    vals = [v1[0:1] + v2[0:SUBLANES], v1[0:1] + v2[SUBLANES:k]]
    ids = [sub, sub + SUBLANES]
    for a in range(1, SUBLANES):
        vals.append(v1[a:a + 1] + v2[0:SUBLANES])
        ids.append(sub + a * k)
    vals.append(v1[SUBLANES:k] + v2[0:1])
    ids.append((sub + SUBLANES) * k)
    return jnp.concatenate(vals, axis=0), jnp.concatenate(ids, axis=0)


def _pick_rows(table, sel):
    out = jnp.zeros_like(table)
    for a in range(PEER_TOPK):
        out = jnp.where(sel == a, table[a:a + 1, :], out)
    return out


def _peer_select_kernel(h_ref, wq_ref, k1_ref, k2_ref, e_ref, g_ref, *, tm, row_words):
    q = jnp.dot(h_ref[...].astype(BF16), wq_ref[...], preferred_element_type=F32)
    k1 = k1_ref[...]
    k2 = k2_ref[...]
    half = k1.shape[1]
    key_ids = lax.broadcasted_iota(jnp.int32, (PEER_N_KEYS, tm), 0)
    for hd in range(PEER_HEADS):
        q1 = q[:, (2 * hd) * half:(2 * hd + 1) * half].astype(BF16)
        q2 = q[:, (2 * hd + 1) * half:(2 * hd + 2) * half].astype(BF16)
        s1 = lax.dot_general(k1, q1, NT_DIMS, preferred_element_type=F32)
        s2 = lax.dot_general(k2, q2, NT_DIMS, preferred_element_type=F32)
        v1, i1 = _top16(s1, key_ids)
        v2, i2 = _top16(s2, key_ids)
        sc, ci = _top16(*_pair_candidates(v1, v2, tm))
        e = (_pick_rows(i1, ci // PEER_TOPK) * PEER_N_KEYS + _pick_rows(i2, ci % PEER_TOPK))
        p = jnp.exp(sc - sc[0:1, :])
        g = p / jnp.sum(p, axis=0, keepdims=True)
        e_ref[hd * PEER_TOPK:(hd + 1) * PEER_TOPK, :] = e * row_words + SUBLANES
        g_ref[hd * PEER_TOPK:(hd + 1) * PEER_TOPK, :] = g


def _peer_select(h2, wq, k1, k2, row_words, tm=256):
    T, D = h2.shape
    n_slots = PEER_HEADS * PEER_TOPK
    const = lambda i: (0, 0)
    return pl.pallas_call(
        functools.partial(_peer_select_kernel, tm=tm, row_words=row_words),
        out_shape=(jax.ShapeDtypeStruct((n_slots, T), jnp.int32),
                   jax.ShapeDtypeStruct((n_slots, T), F32)),
        grid=(T // tm,),
        in_specs=[pl.BlockSpec((tm, D), lambda i: (i, 0)),
                  pl.BlockSpec(wq.shape, const),
                  pl.BlockSpec(k1.shape, const),
                  pl.BlockSpec(k2.shape, const)],
        out_specs=(pl.BlockSpec((n_slots, tm), lambda i: (0, i)),
                   pl.BlockSpec((n_slots, tm), lambda i: (0, i))),
        compiler_params=_vmem_limit(40),
        name="peer_select",
    )(h2, wq, k1, k2)


def _load_table(tbl_hbm, tbl, sem):
    n_words = tbl_hbm.shape[0]

    @pl.when(pl.program_id(0) == 0)
    def _():
        pad = jnp.zeros((SUBLANES, LANES), tbl.dtype)
        tbl[pl.ds(0, SUBLANES), :] = pad
        tbl[pl.ds(SUBLANES + n_words, SUBLANES), :] = pad
        cp = pltpu.make_async_copy(tbl_hbm, tbl.at[pl.ds(SUBLANES, n_words)], sem)
        cp.start()
        cp.wait()


def _pair_words(tbl, e_ref, p, t, lo_half):
    n_pairs = e_ref.shape[1] // 2
    va = tbl[pl.ds(e_ref[t, p], SUBLANES), :]
    vb = tbl[pl.ds(e_ref[t, p + n_pairs] - HALF, SUBLANES), :]
    return jnp.where(lo_half, va, vb)


def _pair_rows(tbl, e_ref, p, t, lo_half):
    pr = _pair_words(tbl, e_ref, p, t, lo_half)
    lo = lax.bitcast_convert_type(pr << 16, F32)
    hi = lax.bitcast_convert_type(pr & jnp.uint32(0xFFFF0000), F32)
    return lo, hi


def _peer_u_kernel(e_ref, x_ref, g_ref, tbl_hbm, w_ref, tbl, sem, *, tb):
    _load_table(tbl_hbm, tbl, sem)
    n_pairs = e_ref.shape[1] // 2
    sub = lax.broadcasted_iota(jnp.int32, (SUBLANES, LANES), 0)
    lane = lax.broadcasted_iota(jnp.int32, (SUBLANES, LANES), 1)
    lo_half = sub < HALF

    def token(t):
        xt = x_ref[t]
        xr = pltpu.roll(xt, HALF, axis=0)
        x_lo = jnp.where(lo_half, xt, xr)
        x_hi = jnp.where(lo_half, xr, xt)
        tile = jnp.zeros((SUBLANES, LANES), F32)
        for p in range(n_pairs):
            lo, hi = _pair_rows(tbl, e_ref, p, t, lo_half)
            part = jnp.sum(lo * x_lo + hi * x_hi, axis=-1, keepdims=True)
            tile = jnp.where(lane == p, part, tile)
        tile = tile + pltpu.roll(tile, SUBLANES - 1, axis=0)
        tile = tile + pltpu.roll(tile, SUBLANES - 2, axis=0)
        w_ref[pl.ds(t, 1), :] = jnp.where(lane[0:1] < n_pairs, tile[0:1],
                                          pltpu.roll(tile, n_pairs, axis=1)[HALF:HALF + 1])

    def group(tg, carry):
        for u in range(TOKENS_PER_ITER):
            token(tg * TOKENS_PER_ITER + u)
        return carry

    lax.fori_loop(0, tb // TOKENS_PER_ITER, group, 0)
    w_ref[...] = g_ref[...] * jax.nn.gelu(w_ref[...])


def _peer_v_kernel(e_ref, w_ref, h_ref, tbl_hbm, g_ref, b_ref, o_ref, tbl, sem, *, tb, alpha):
    _load_table(tbl_hbm, tbl, sem)
    n_pairs = e_ref.shape[1] // 2
    n_rows = w_ref.shape[1]
    sub = lax.broadcasted_iota(jnp.int32, (SUBLANES, LANES), 0)
    lo_half = sub < HALF
    n_feat = SUBLANES * LANES
    own_chunk = (lax.broadcasted_iota(jnp.int32, (SUBLANES, n_rows), 1) % SUBLANES
                 == lax.broadcasted_iota(jnp.int32, (SUBLANES, n_rows), 0))

    def token(t):
        rows = [pltpu.bitcast(_pair_words(tbl, e_ref, p, t, lo_half), BF16) for p in range(n_pairs)]
        stacked = jnp.concatenate(rows, axis=0)
        w_row = jnp.broadcast_to(w_ref[pl.ds(t, 1), :], (SUBLANES, n_rows))
        w_mat = jnp.where(own_chunk, w_row, 0.0).astype(BF16)
        o_ref[t] = jnp.dot(w_mat, stacked, preferred_element_type=F32)

    def group(tg, carry):
        for u in range(TOKENS_PER_ITER):
            token(tg * TOKENS_PER_ITER + u)
        return carry

    lax.fori_loop(0, tb // TOKENS_PER_ITER, group, 0)

    y = alpha * h_ref[...] + o_ref[...]
    mu = jnp.sum(jnp.sum(y, axis=2, keepdims=True), axis=1, keepdims=True) / n_feat
    d = y - mu
    var = jnp.sum(jnp.sum(d * d, axis=2, keepdims=True), axis=1, keepdims=True) / n_feat
    o_ref[...] = d * lax.rsqrt(var + LN_EPS) * g_ref[...] + b_ref[...]


def _pack_table(tab):
    E, D = tab.shape
    bits = lax.bitcast_convert_type(tab.astype(BF16), jnp.uint16).astype(jnp.uint32)
    packed = bits[:, :D // 2] | (bits[:, D // 2:] << 16)
    return packed.reshape(E * (D // 2) // LANES, LANES)


def _peer_scratch(tbl_words):
    return [pltpu.VMEM((tbl_words.shape[0] + 2 * SUBLANES, LANES), jnp.uint32), pltpu.SemaphoreType.DMA]


def _peer_u(e_tok, h_tiles, g_tok, tbl_words, tb=128):
    T, n_slots = e_tok.shape
    return pl.pallas_call(
        functools.partial(_peer_u_kernel, tb=tb),
        out_shape=jax.ShapeDtypeStruct((T, n_slots), F32),
        grid=(T // tb,),
        in_specs=[
            pl.BlockSpec((tb, n_slots), lambda i: (i, 0), memory_space=pltpu.SMEM),
            pl.BlockSpec((tb, SUBLANES, LANES), lambda i: (i, 0, 0)),
            pl.BlockSpec((tb, n_slots), lambda i: (i, 0)),
            pl.BlockSpec(memory_space=pl.ANY),
        ],
        out_specs=pl.BlockSpec((tb, n_slots), lambda i: (i, 0)),
        scratch_shapes=_peer_scratch(tbl_words),
        compiler_params=_vmem_limit(48),
        name="peer_u",
    )(e_tok, h_tiles, g_tok, tbl_words)


def _peer_v(e_tok, w_rows, h_chunks, tbl_words, ln_g, ln_b, alpha, tb=128):
    T, n_slots = e_tok.shape
    n_rows = w_rows.shape[1]
    const = lambda i: (0, 0)
    return pl.pallas_call(
        functools.partial(_peer_v_kernel, tb=tb, alpha=alpha),
        out_shape=jax.ShapeDtypeStruct((T, SUBLANES, LANES), F32),
        grid=(T // tb,),
        in_specs=[
            pl.BlockSpec((tb, n_slots), lambda i: (i, 0), memory_space=pltpu.SMEM),
            pl.BlockSpec((tb, n_rows), lambda i: (i, 0)),
            pl.BlockSpec((tb, SUBLANES, LANES), lambda i: (i, 0, 0)),
            pl.BlockSpec(memory_space=pl.ANY),
            pl.BlockSpec((SUBLANES, LANES), const),
            pl.BlockSpec((SUBLANES, LANES), const),
        ],
        out_specs=pl.BlockSpec((tb, SUBLANES, LANES), lambda i: (i, 0, 0)),
        scratch_shapes=_peer_scratch(tbl_words),
        compiler_params=_vmem_limit(48),
        name="peer_v",
    )(e_tok, w_rows, h_chunks, tbl_words, ln_g, ln_b)


def _to_chunk_order(a):
    lead = a.shape[:-1]
    return a.reshape(*lead, 2, HALF, LANES).swapaxes(-3, -2).reshape(*lead, SUBLANES, LANES)


def _from_chunk_order(a):
    lead = a.shape[:-2]
    return a.reshape(*lead, HALF, 2, LANES).swapaxes(-3, -2).reshape(*lead, SUBLANES * LANES)


def kernel(x, w_in, lambda_q1, lambda_k1, lambda_q2, lambda_k2, diff_subln_g, rel_bias, w_branch_a, w_branch_b, w_out, ln1_g, ln1_b, peer_wq, peer_k1, peer_k2, peer_u, peer_v, ln2_g, ln2_b):
    B, S, D = x.shape
    T = B * S
    depth = w_in.shape[0]
    alpha = (2.0 * depth) ** 0.25
    tq = 256
    assert D == SUBLANES * LANES and S % tq == 0 and tq % CHUNK == 0

    sb_w = SB_HEADS * SB_HEAD_DIM
    df_w = DF_HEADS * 2 * DF_HEAD_DIM
    assert SB_HEAD_DIM == 64 and DF_HEAD_DIM == 64
    col_scale = jnp.ones((w_in.shape[2],), F32)
    col_scale = col_scale.at[:sb_w].set(0.125).at[3 * sb_w:3 * sb_w + df_w].set(0.125)

    r = jnp.arange(tq, dtype=jnp.int32)
    later = (r[:, None] > r[None, :]).astype(BF16)
    mext = jnp.concatenate([later, jnp.ones((tq, LANES), BF16)], axis=1)
    bias_tiles, far_bias = _df_bias_tiles(rel_bias, tq)
    row_words = D // 2 // LANES
    n_slots = PEER_HEADS * PEER_TOPK

    h = x.reshape(T, D)
    for l in range(depth):
        lam_init = 0.8 - 0.6 * math.exp(-0.3 * l)
        proj = _inproj(h, (w_in[l] * col_scale).astype(BF16))
        o_sb = _sb_attention(proj, mext, B, S, tq)
        lam_vec = jnp.stack([lambda_q1[l], lambda_k1[l], lambda_q2[l], lambda_k2[l]]).astype(F32)
        o_df = _df_attention(proj, lam_vec, bias_tiles, far_bias, diff_subln_g[l].reshape(1, -1).astype(F32),
                             B, S, tq, lam_init)
        h = _merge(o_sb, o_df, proj, h, w_branch_a[l].astype(BF16), w_branch_b[l].astype(BF16),
                   w_out[l].astype(BF16), ln1_g[l].reshape(1, D), ln1_b[l].reshape(1, D), alpha)
        e_t, g_t = _peer_select(h, peer_wq[l].astype(BF16), peer_k1[l].astype(BF16),
                                peer_k2[l].astype(BF16), row_words)
        e_tok = e_t.T
        w_tok = _peer_u(e_tok, h.reshape(T, SUBLANES, LANES), g_t.T, _pack_table(peer_u[l]))
        w_rows = jnp.repeat(w_tok.reshape(T, 2, n_slots // 2).swapaxes(1, 2).reshape(T, n_slots),
                            SUBLANES, axis=1)
        h = _from_chunk_order(_peer_v(e_tok, w_rows, _to_chunk_order(h), _pack_table(peer_v[l]),
                                      _to_chunk_order(ln2_g[l]), _to_chunk_order(ln2_b[l]), alpha))
    return h.reshape(B, S, D)
```

```python
import functools
import math

import jax
import jax.numpy as jnp
from jax import lax
from jax.experimental import pallas as pl
from jax.experimental.pallas import tpu as pltpu

F32 = jnp.float32
BF16 = jnp.bfloat16

CHUNK = 64
SB_HEADS = 8
SB_HEAD_DIM = 64
DF_HEADS = 4
DF_HEAD_DIM = 64
N_BUCKETS = 32
MAX_DISTANCE = 256
PEER_HEADS = 8
PEER_N_KEYS = 128
PEER_TOPK = 16
LN_EPS = 1e-5

LANES = 128
SUBLANES = 8
HALF = SUBLANES // 2
PEER_TOKEN_BLOCK = 64
PAIRS_PER_DOT = 16
NEG_BIG = -1e30
EXP_ZERO_BELOW = -104.0

NT_DIMS = (((1,), (1,)), ((), ()))


def _vmem_limit(mib):
    return pltpu.CompilerParams(vmem_limit_bytes=mib * 1024 * 1024)


def _inproj_kernel(x_ref, w_ref, o_ref):
    o_ref[...] = jnp.dot(x_ref[...].astype(BF16), w_ref[...],
                         preferred_element_type=F32).astype(o_ref.dtype)


def _inproj(x2, w_bf16, tm=512, tn=1024):
    T, K = x2.shape
    N = w_bf16.shape[1]
    return pl.pallas_call(
        _inproj_kernel,
        out_shape=jax.ShapeDtypeStruct((T, N), BF16),
        grid=(T // tm, N // tn),
        in_specs=[pl.BlockSpec((tm, K), lambda i, j: (i, 0)),
                  pl.BlockSpec((K, tn), lambda i, j: (0, j))],
        out_specs=pl.BlockSpec((tm, tn), lambda i, j: (i, j)),
        compiler_params=_vmem_limit(40),
        name="in_proj",
    )(x2, w_bf16)


def _sb_kernel(q_ref, k_ref, v_ref, m_ref, o_ref, acc_ref, r_ref, *, tq):
    i = pl.program_id(2)
    q = q_ref[...]
    lane = lax.broadcasted_iota(jnp.int32, (tq, LANES), 1)
    zero = jnp.zeros_like(q)
    q_heads = (jnp.where(lane < SB_HEAD_DIM, q, zero), jnp.where(lane >= SB_HEAD_DIM, q, zero))
    row = lax.broadcasted_iota(jnp.int32, (tq, tq), 0)
    col = lax.broadcasted_iota(jnp.int32, (tq, tq), 1)
    strict = col < row

    acc_ref[...] = jnp.zeros_like(acc_ref)
    r_ref[...] = jnp.zeros_like(r_ref)

    def visit(j, diag):
        start = pl.multiple_of(j * tq, tq)
        kb = k_ref[pl.ds(start, tq), :]
        vb = v_ref[pl.ds(start, tq), :]
        mext = m_ref[...]
        for h in range(2):
            z = lax.dot_general(q_heads[h], kb, NT_DIMS, preferred_element_type=F32)
            log_beta = jnp.minimum(z, 0.0) - jnp.log(1.0 + jnp.exp(-jnp.abs(z)))
            log_fail = log_beta - z
            if diag:
                log_fail = jnp.where(strict, log_fail, 0.0)
            hi = log_fail.astype(BF16)
            lo = (log_fail - hi.astype(F32)).astype(BF16)
            cum = (jnp.dot(hi, mext, preferred_element_type=F32)
                   + jnp.dot(lo, mext, preferred_element_type=F32))
            r_old = r_ref[h]
            arg = log_beta + cum[:, :tq] + jnp.tile(r_old, (1, tq // LANES))
            w = jnp.exp(arg)
            if diag:
                w = jnp.where(strict, w, 0.0)
            acc_ref[h] += jnp.dot(w.astype(BF16), vb, preferred_element_type=F32)
            r_ref[h] = r_old + cum[:, tq:]

    visit(i, True)

    def cond(c):
        j, live = c
        return jnp.logical_and(j >= 0, live > 0)

    def body(c):
        j, _ = c
        visit(j, False)
        r_max = jnp.max(jnp.maximum(r_ref[0], r_ref[1]))
        return j - 1, (r_max >= EXP_ZERO_BELOW).astype(jnp.int32)

    lax.while_loop(cond, body, (i - 1, jnp.int32(1)))
    o_ref[...] = jnp.where(lane < SB_HEAD_DIM, acc_ref[0], acc_ref[1]).astype(o_ref.dtype)


def _sb_attention(proj, mext, B, S, tq):
    n_pairs = SB_HEADS * SB_HEAD_DIM // LANES
    nq = S // tq
    return pl.pallas_call(
        functools.partial(_sb_kernel, tq=tq),
        out_shape=jax.ShapeDtypeStruct((B * S, n_pairs * LANES), BF16),
        grid=(B, n_pairs, nq),
        in_specs=[
            pl.BlockSpec((tq, LANES), lambda b, p, i: (b * nq + i, p)),
            pl.BlockSpec((S, LANES), lambda b, p, i: (b, n_pairs + p)),
            pl.BlockSpec((S, LANES), lambda b, p, i: (b, 2 * n_pairs + p)),
            pl.BlockSpec((tq, tq + LANES), lambda b, p, i: (0, 0)),
        ],
        out_specs=pl.BlockSpec((tq, LANES), lambda b, p, i: (b * nq + i, p)),
        scratch_shapes=[pltpu.VMEM((2, tq, LANES), F32), pltpu.VMEM((2, tq, LANES), F32)],
        compiler_params=_vmem_limit(40),
        name="sb_attn",
    )(proj, proj, proj, mext)


def _df_kernel(far_ref, lam_ref, q_ref, k_ref, v_ref, bias_ref, g_ref, o_ref, acc_ref, m_ref,
               *, tq, n_near, lam_init):
    i = pl.program_id(2)
    q = q_ref[...]
    lane = lax.broadcasted_iota(jnp.int32, (tq, LANES), 1)
    zero = jnp.zeros_like(q)
    q_maps = (jnp.where(lane < DF_HEAD_DIM, q, zero), jnp.where(lane >= DF_HEAD_DIM, q, zero))

    acc_ref[...] = jnp.zeros_like(acc_ref)
    m_ref[...] = jnp.full_like(m_ref, NEG_BIG)

    def visit(start, width, bias, shift):
        kb = k_ref[pl.ds(start, width), :]
        vb = v_ref[pl.ds(start, width), :]
        v_ext = jnp.concatenate([vb, jnp.ones((width, LANES), BF16)], axis=1)
        for m in range(2):
            s = lax.dot_general(q_maps[m], kb, NT_DIMS, preferred_element_type=F32)
            if bias is not None:
                s = s + bias
            m_old = m_ref[m]
            m_new = jnp.maximum(m_old, jnp.max(s, axis=-1, keepdims=True) + shift)
            p = jnp.exp(s - jnp.tile(m_new - shift, (1, width // LANES)))
            corr = jnp.exp(m_old - m_new)
            acc_ref[m] = (jnp.tile(corr, (1, 2)) * acc_ref[m]
                          + jnp.dot(p.astype(BF16), v_ext, preferred_element_type=F32))
            m_ref[m] = m_new

    far_bias = far_ref[pl.program_id(1)]
    n_far = jnp.maximum(i - n_near + 1, 0)

    def far_pair(jj, carry):
        visit(pl.multiple_of(jj * 2 * tq, 2 * tq), 2 * tq, None, far_bias)
        return carry

    lax.fori_loop(0, n_far // 2, far_pair, 0)

    @pl.when(n_far % 2 == 1)
    def _():
        visit(pl.multiple_of((n_far - 1) * tq, tq), tq, None, far_bias)

    for d in range(n_near - 1, -1, -1):
        @pl.when(i >= d)
        def _():
            visit(pl.multiple_of((i - d) * tq, tq), tq, bias_ref[0, d], 0.0)

    lam_vec = lam_ref[...]
    lam = (jnp.exp(jnp.sum(lam_vec[0:1] * lam_vec[1:2], axis=-1, keepdims=True))
           - jnp.exp(jnp.sum(lam_vec[2:3] * lam_vec[3:4], axis=-1, keepdims=True)) + lam_init)
    o = (acc_ref[0, :, :LANES] / acc_ref[0, :, LANES:]
         - lam * (acc_ref[1, :, :LANES] / acc_ref[1, :, LANES:]))
    o = o * lax.rsqrt(jnp.mean(o * o, axis=-1, keepdims=True) + LN_EPS)
    o_ref[...] = (o * g_ref[...] * (1.0 - lam_init)).astype(o_ref.dtype)


def _df_attention(proj, lam_vec, bias_tiles, far_bias, subln_g, B, S, tq, lam_init):
    nq = S // tq
    n_near = bias_tiles.shape[1]
    q0 = 3 * SB_HEADS * SB_HEAD_DIM // LANES
    return pl.pallas_call(
        functools.partial(_df_kernel, tq=tq, n_near=n_near, lam_init=lam_init),
        out_shape=jax.ShapeDtypeStruct((B * S, DF_HEADS * LANES), BF16),
        grid=(B, DF_HEADS, nq),
        in_specs=[
            pl.BlockSpec(memory_space=pltpu.SMEM),
            pl.BlockSpec((4, DF_HEAD_DIM), lambda b, h, i: (0, 0)),
            pl.BlockSpec((tq, LANES), lambda b, h, i: (b * nq + i, q0 + h)),
            pl.BlockSpec((S, LANES), lambda b, h, i: (b, q0 + DF_HEADS + h)),
            pl.BlockSpec((S, LANES), lambda b, h, i: (b, q0 + 2 * DF_HEADS + h)),
            pl.BlockSpec((1, n_near, tq, tq), lambda b, h, i: (h, 0, 0, 0)),
            pl.BlockSpec((1, LANES), lambda b, h, i: (0, 0)),
        ],
        out_specs=pl.BlockSpec((tq, LANES), lambda b, h, i: (b * nq + i, h)),
        scratch_shapes=[pltpu.VMEM((2, tq, 2 * LANES), F32), pltpu.VMEM((2, tq, LANES), F32)],
        compiler_params=_vmem_limit(40),
        name="df_attn",
    )(far_bias, lam_vec, proj, proj, proj, bias_tiles, subln_g)


def _t5_bucket(rel):
    half = N_BUCKETS // 2
    max_exact = half // 2
    ret = (rel > 0).astype(jnp.int32) * half
    n = jnp.abs(rel)
    nf = jnp.maximum(n, 1).astype(F32)
    large = max_exact + (jnp.log(nf / max_exact) / math.log(MAX_DISTANCE / max_exact)
                         * (half - max_exact)).astype(jnp.int32)
    large = jnp.minimum(large, half - 1)
    return ret + jnp.where(n < max_exact, n, large)


def _df_bias_tiles(rel_bias, tq):
    n_near = -(-(MAX_DISTANCE - 1) // tq) + 1
    r = jnp.arange(tq, dtype=jnp.int32)[:, None]
    c = jnp.arange(tq, dtype=jnp.int32)[None, :]
    tiles = []
    for d in range(n_near):
        b = rel_bias[_t5_bucket(c - r - d * tq)].astype(F32)
        if d == 0:
            allowed = (c // CHUNK) <= (r // CHUNK)
            b = jnp.where(allowed[:, :, None], b, NEG_BIG)
        tiles.append(b.transpose(2, 0, 1))
    far = rel_bias[_t5_bucket(jnp.int32(-MAX_DISTANCE))].astype(F32)
    return jnp.stack(tiles, axis=1), far


def _layer_norm(y, g, b):
    mu = jnp.mean(y, axis=-1, keepdims=True)
    d = y - mu
    var = jnp.mean(d * d, axis=-1, keepdims=True)
    return d * lax.rsqrt(var + LN_EPS) * g + b


def _merge_kernel(osb_ref, odf_ref, ga_ref, gb_ref, x_ref, wa_ref, wb_ref, wo_ref, g_ref, b_ref,
                  h_ref, *, alpha):
    a = jnp.dot(osb_ref[...], wa_ref[...], preferred_element_type=F32)
    b = jnp.dot(odf_ref[...], wb_ref[...], preferred_element_type=F32)
    merged = (jax.nn.sigmoid(ga_ref[...].astype(F32)) * a
              + jax.nn.sigmoid(gb_ref[...].astype(F32)) * b)
    mix = jnp.dot(merged.astype(BF16), wo_ref[...], preferred_element_type=F32)
    h_ref[...] = _layer_norm(alpha * x_ref[...] + mix, g_ref[...], b_ref[...])


def _merge(o_sb, o_df, proj, x2, w_a, w_b, w_o, ln_g, ln_b, alpha, tm=512):
    T, D = x2.shape
    Wa, Wb = o_sb.shape[1], o_df.shape[1]
    gate_blk = (proj.shape[1] - 2 * D) // D
    const = lambda i: (0, 0)
    return pl.pallas_call(
        functools.partial(_merge_kernel, alpha=alpha),
        out_shape=jax.ShapeDtypeStruct((T, D), F32),
        grid=(T // tm,),
        in_specs=[
            pl.BlockSpec((tm, Wa), lambda i: (i, 0)),
            pl.BlockSpec((tm, Wb), lambda i: (i, 0)),
            pl.BlockSpec((tm, D), lambda i: (i, gate_blk)),
            pl.BlockSpec((tm, D), lambda i: (i, gate_blk + 1)),
            pl.BlockSpec((tm, D), lambda i: (i, 0)),
            pl.BlockSpec((Wa, D), const),
            pl.BlockSpec((Wb, D), const),
            pl.BlockSpec((D, D), const),
            pl.BlockSpec((1, D), const),
            pl.BlockSpec((1, D), const),
        ],
        out_specs=pl.BlockSpec((tm, D), lambda i: (i, 0)),
        compiler_params=_vmem_limit(40),
        name="merge_ln1",
    )(o_sb, o_df, proj, proj, x2, w_a, w_b, w_o, ln_g, ln_b)


def _top16(s, ids):
    id_bound = jnp.iinfo(jnp.int32).max
    vals, idxs = [], []
    for _ in range(PEER_TOPK):
        m = jnp.max(s, axis=0, keepdims=True)
        idx = jnp.min(jnp.where(s == m, ids, id_bound), axis=0, keepdims=True)
        vals.append(m)
        idxs.append(idx)
        s = jnp.where(ids == idx, -jnp.inf, s)
    return jnp.concatenate(vals, axis=0), jnp.concatenate(idxs, axis=0)


def _pair_candidates(v1, v2, tm):
    sub = lax.broadcasted_iota(jnp.int32, (SUBLANES, tm), 0)
    k = PEER_TOPK
    vals = [v1[0:1] + v2[0:SUBLANES], v1[0:1] + v2[SUBLANES:k]]
    ids = [sub, sub + SUBLANES]
    for a in range(1, SUBLANES):
        vals.append(v1[a:a + 1] + v2[0:SUBLANES])
        ids.append(sub + a * k)
    vals.append(v1[SUBLANES:k] + v2[0:1])
    ids.append((sub + SUBLANES) * k)
    return jnp.concatenate(vals, axis=0), jnp.concatenate(ids, axis=0)


def _pick_rows(table, sel):
    out = jnp.zeros_like(table)
    for a in range(PEER_TOPK):
        out = jnp.where(sel == a, table[a:a + 1, :], out)
    return out


def _peer_select_kernel(h_ref, wq_ref, k1_ref, k2_ref, e_ref, g_ref, *, tm, row_words):
    q = jnp.dot(h_ref[...].astype(BF16), wq_ref[...], preferred_element_type=F32)
    k1 = k1_ref[...]
    k2 = k2_ref[...]
    half = k1.shape[1]
    key_ids = lax.broadcasted_iota(jnp.int32, (PEER_N_KEYS, tm), 0)
    for hd in range(PEER_HEADS):
        q1 = q[:, (2 * hd) * half:(2 * hd + 1) * half].astype(BF16)
        q2 = q[:, (2 * hd + 1) * half:(2 * hd + 2) * half].astype(BF16)
        s1 = lax.dot_general(k1, q1, NT_DIMS, preferred_element_type=F32)
        s2 = lax.dot_general(k2, q2, NT_DIMS, preferred_element_type=F32)
        v1, i1 = _top16(s1, key_ids)
        v2, i2 = _top16(s2, key_ids)
        sc, ci = _top16(*_pair_candidates(v1, v2, tm))
        e = (_pick_rows(i1, ci // PEER_TOPK) * PEER_N_KEYS + _pick_rows(i2, ci % PEER_TOPK))
        p = jnp.exp(sc - sc[0:1, :])
        g = p / jnp.sum(p, axis=0, keepdims=True)
        e_ref[hd * PEER_TOPK:(hd + 1) * PEER_TOPK, :] = e * row_words + SUBLANES
        g_ref[hd * PEER_TOPK:(hd + 1) * PEER_TOPK, :] = g


def _peer_select(h2, wq, k1, k2, row_words, tm=256):
    T, D = h2.shape
    n_slots = PEER_HEADS * PEER_TOPK
    const = lambda i: (0, 0)
    return pl.pallas_call(
        functools.partial(_peer_select_kernel, tm=tm, row_words=row_words),
        out_shape=(jax.ShapeDtypeStruct((n_slots, T), jnp.int32),
                   jax.ShapeDtypeStruct((n_slots, T), F32)),
        grid=(T // tm,),
        in_specs=[pl.BlockSpec((tm, D), lambda i: (i, 0)),
                  pl.BlockSpec(wq.shape, const),
                  pl.BlockSpec(k1.shape, const),
                  pl.BlockSpec(k2.shape, const)],
        out_specs=(pl.BlockSpec((n_slots, tm), lambda i: (0, i)),
                   pl.BlockSpec((n_slots, tm), lambda i: (0, i))),
        compiler_params=_vmem_limit(40),
        name="peer_select",
    )(h2, wq, k1, k2)


def _load_table(tbl_hbm, tbl, sem):
    n_words = tbl_hbm.shape[0]

    @pl.when(pl.program_id(0) == 0)
    def _():
        pad = jnp.zeros((SUBLANES, LANES), tbl.dtype)
        tbl[pl.ds(0, SUBLANES), :] = pad
        tbl[pl.ds(SUBLANES + n_words, SUBLANES), :] = pad
        cp = pltpu.make_async_copy(tbl_hbm, tbl.at[pl.ds(SUBLANES, n_words)], sem)
        cp.start()
        cp.wait()


def _pair_words(tbl, e_ref, p, t, lo_half):
    n_pairs = e_ref.shape[1] // 2
    va = tbl[pl.ds(e_ref[t, p], SUBLANES), :]
    vb = tbl[pl.ds(e_ref[t, p + n_pairs] - HALF, SUBLANES), :]
    return jnp.where(lo_half, va, vb)


def _pair_rows(tbl, e_ref, p, t, lo_half):
    pr = _pair_words(tbl, e_ref, p, t, lo_half)
    lo = lax.bitcast_convert_type(pr << 16, F32)
    hi = lax.bitcast_convert_type(pr & jnp.uint32(0xFFFF0000), F32)
    return lo, hi


def _peer_u_kernel(e_ref, x_ref, g_ref, tbl_hbm, w_ref, tbl, sem, *, tb):
    _load_table(tbl_hbm, tbl, sem)
    n_pairs = e_ref.shape[1] // 2
    sub = lax.broadcasted_iota(jnp.int32, (SUBLANES, LANES), 0)
    lane = lax.broadcasted_iota(jnp.int32, (SUBLANES, LANES), 1)
    lo_half = sub < HALF
    ones = jnp.ones((LANES, LANES), BF16)

    def token(t):
        xt = x_ref[t]
        xr = pltpu.roll(xt, HALF, axis=0)
        x_lo = jnp.where(lo_half, xt, xr)
        x_hi = jnp.where(lo_half, xr, xt)
        tile = jnp.zeros((SUBLANES, LANES), F32)
        for p0 in range(0, n_pairs, PAIRS_PER_DOT):
            prods = []
            for p in range(p0, p0 + PAIRS_PER_DOT):
                lo, hi = _pair_rows(tbl, e_ref, p, t, lo_half)
                prods.append(lo * x_lo + hi * x_hi)
            sums = jnp.dot(jnp.concatenate(prods, axis=0).astype(BF16), ones,
                           preferred_element_type=F32)
            for k in range(PAIRS_PER_DOT):
                tile = jnp.where(lane == p0 + k, sums[k * SUBLANES:(k + 1) * SUBLANES], tile)
        tile = tile + pltpu.roll(tile, SUBLANES - 1, axis=0)
        tile = tile + pltpu.roll(tile, SUBLANES - 2, axis=0)
        w_ref[pl.ds(t, 1), :] = jnp.where(lane[0:1] < n_pairs, tile[0:1],
                                          pltpu.roll(tile, n_pairs, axis=1)[HALF:HALF + 1])

    for t in range(tb):
        token(t)
    w_ref[...] = g_ref[...] * jax.nn.gelu(w_ref[...])


def _peer_v_kernel(e_ref, w_ref, h_ref, tbl_hbm, g_ref, b_ref, o_ref, tbl, sem, *, tb, alpha):
    _load_table(tbl_hbm, tbl, sem)
    n_pairs = e_ref.shape[1] // 2
    n_rows = w_ref.shape[1]
    sub = lax.broadcasted_iota(jnp.int32, (SUBLANES, LANES), 0)
    lo_half = sub < HALF
    n_feat = SUBLANES * LANES
    r = lax.broadcasted_iota(jnp.int32, (SUBLANES, n_rows), 1)
    own_chunk = (HALF * (r % 2) + (r // 2) % HALF
                 == lax.broadcasted_iota(jnp.int32, (SUBLANES, n_rows), 0))

    def token(t):
        rows = [pltpu.bitcast(_pair_words(tbl, e_ref, p, t, lo_half), BF16) for p in range(n_pairs)]
        stacked = jnp.concatenate(rows, axis=0)
        w_row = jnp.broadcast_to(w_ref[pl.ds(t, 1), :], (SUBLANES, n_rows))
        w_mat = jnp.where(own_chunk, w_row, 0.0).astype(BF16)
        o_ref[t] = jnp.dot(w_mat, stacked, preferred_element_type=F32)

    for t in range(tb):
        token(t)

    y = alpha * h_ref[...] + o_ref[...]
    mu = jnp.sum(jnp.sum(y, axis=2, keepdims=True), axis=1, keepdims=True) / n_feat
    d = y - mu
    var = jnp.sum(jnp.sum(d * d, axis=2, keepdims=True), axis=1, keepdims=True) / n_feat
    o_ref[...] = d * lax.rsqrt(var + LN_EPS) * g_ref[...] + b_ref[...]


def _pack_table(tab):
    E, D = tab.shape
    bits = lax.bitcast_convert_type(tab.astype(BF16), jnp.uint16).astype(jnp.uint32)
    packed = bits[:, :D // 2] | (bits[:, D // 2:] << 16)
    return packed.reshape(E * (D // 2) // LANES, LANES)


def _peer_scratch(tbl_words):
    return [pltpu.VMEM((tbl_words.shape[0] + 2 * SUBLANES, LANES), jnp.uint32), pltpu.SemaphoreType.DMA]


def _peer_u(e_tok, h_tiles, g_tok, tbl_words, tb=PEER_TOKEN_BLOCK):
    T, n_slots = e_tok.shape
    return pl.pallas_call(
        functools.partial(_peer_u_kernel, tb=tb),
        out_shape=jax.ShapeDtypeStruct((T, n_slots), F32),
        grid=(T // tb,),
        in_specs=[
            pl.BlockSpec((tb, n_slots), lambda i: (i, 0), memory_space=pltpu.SMEM),
            pl.BlockSpec((tb, SUBLANES, LANES), lambda i: (i, 0, 0)),
            pl.BlockSpec((tb, n_slots), lambda i: (i, 0)),
            pl.BlockSpec(memory_space=pl.ANY),
        ],
        out_specs=pl.BlockSpec((tb, n_slots), lambda i: (i, 0)),
        scratch_shapes=_peer_scratch(tbl_words),
        compiler_params=_vmem_limit(48),
        name="peer_u",
    )(e_tok, h_tiles, g_tok, tbl_words)


def _peer_v(e_tok, w_rows, h_tiles, tbl_words, ln_g, ln_b, alpha, tb=PEER_TOKEN_BLOCK):
    T, n_slots = e_tok.shape
    n_rows = w_rows.shape[1]
    const = lambda i: (0, 0)
    return pl.pallas_call(
        functools.partial(_peer_v_kernel, tb=tb, alpha=alpha),
        out_shape=jax.ShapeDtypeStruct((T, SUBLANES, LANES), F32),
        grid=(T // tb,),
        in_specs=[
            pl.BlockSpec((tb, n_slots), lambda i: (i, 0), memory_space=pltpu.SMEM),
            pl.BlockSpec((tb, n_rows), lambda i: (i, 0)),
            pl.BlockSpec((tb, SUBLANES, LANES), lambda i: (i, 0, 0)),
            pl.BlockSpec(memory_space=pl.ANY),
            pl.BlockSpec((SUBLANES, LANES), const),
            pl.BlockSpec((SUBLANES, LANES), const),
        ],
        out_specs=pl.BlockSpec((tb, SUBLANES, LANES), lambda i: (i, 0, 0)),
        scratch_shapes=_peer_scratch(tbl_words),
        compiler_params=_vmem_limit(48),
        name="peer_v",
    )(e_tok, w_rows, h_tiles, tbl_words, ln_g, ln_b)


def kernel(x, w_in, lambda_q1, lambda_k1, lambda_q2, lambda_k2, diff_subln_g, rel_bias, w_branch_a, w_branch_b, w_out, ln1_g, ln1_b, peer_wq, peer_k1, peer_k2, peer_u, peer_v, ln2_g, ln2_b):
    B, S, D = x.shape
    T = B * S
    depth = w_in.shape[0]
    alpha = (2.0 * depth) ** 0.25
    tq = 256
    assert D == SUBLANES * LANES and S % tq == 0 and tq % CHUNK == 0

    sb_w = SB_HEADS * SB_HEAD_DIM
    df_w = DF_HEADS * 2 * DF_HEAD_DIM
    assert SB_HEAD_DIM == 64 and DF_HEAD_DIM == 64
    col_scale = jnp.ones((w_in.shape[2],), F32)
    col_scale = col_scale.at[:sb_w].set(0.125).at[3 * sb_w:3 * sb_w + df_w].set(0.125)

    r = jnp.arange(tq, dtype=jnp.int32)
    later = (r[:, None] > r[None, :]).astype(BF16)
    mext = jnp.concatenate([later, jnp.ones((tq, LANES), BF16)], axis=1)
    bias_tiles, far_bias = _df_bias_tiles(rel_bias, tq)
    row_words = D // 2 // LANES
    n_slots = PEER_HEADS * PEER_TOPK

    h = x.reshape(T, D)
    for l in range(depth):
        lam_init = 0.8 - 0.6 * math.exp(-0.3 * l)
        proj = _inproj(h, (w_in[l] * col_scale).astype(BF16))
        o_sb = _sb_attention(proj, mext, B, S, tq)
        lam_vec = jnp.stack([lambda_q1[l], lambda_k1[l], lambda_q2[l], lambda_k2[l]]).astype(F32)
        o_df = _df_attention(proj, lam_vec, bias_tiles, far_bias, diff_subln_g[l].reshape(1, -1).astype(F32),
                             B, S, tq, lam_init)
        h = _merge(o_sb, o_df, proj, h, w_branch_a[l].astype(BF16), w_branch_b[l].astype(BF16),
                   w_out[l].astype(BF16), ln1_g[l].reshape(1, D), ln1_b[l].reshape(1, D), alpha)
        e_t, g_t = _peer_select(h, peer_wq[l].astype(BF16), peer_k1[l].astype(BF16),
                                peer_k2[l].astype(BF16), row_words)
        e_tok = e_t.T
        h_tiles = h.reshape(T, SUBLANES, LANES)
        w_tok = _peer_u(e_tok, h_tiles, g_t.T, _pack_table(peer_u[l]))
        w_rows = jnp.repeat(w_tok.reshape(T, 2, n_slots // 2).swapaxes(1, 2).reshape(T, n_slots),
                            SUBLANES, axis=1)
        h = _peer_v(e_tok, w_rows, h_tiles, _pack_table(peer_v[l]), ln2_g[l].reshape(SUBLANES, LANES),
                    ln2_b[l].reshape(SUBLANES, LANES), alpha).reshape(T, D)
    return h.reshape(B, S, D)
```

```python
import functools
import math

import jax
import jax.numpy as jnp
from jax import lax
from jax.experimental import pallas as pl
from jax.experimental.pallas import tpu as pltpu

F32 = jnp.float32
BF16 = jnp.bfloat16

CHUNK = 64
SB_HEADS = 8
SB_HEAD_DIM = 64
DF_HEADS = 4
DF_HEAD_DIM = 64
N_BUCKETS = 32
MAX_DISTANCE = 256
PEER_HEADS = 8
PEER_N_KEYS = 128
PEER_TOPK = 16
LN_EPS = 1e-5

LANES = 128
SUBLANES = 8
HALF = SUBLANES // 2
PEER_TOKEN_BLOCK = 64
PAIRS_PER_DOT = 16
NEG_BIG = -1e30
EXP_ZERO_BELOW = -104.0

NT_DIMS = (((1,), (1,)), ((), ()))


def _vmem_limit(mib):
    return pltpu.CompilerParams(vmem_limit_bytes=mib * 1024 * 1024)


def _inproj_kernel(x_ref, w_ref, o_ref):
    o_ref[...] = jnp.dot(x_ref[...].astype(BF16), w_ref[...],
                         preferred_element_type=F32).astype(o_ref.dtype)


def _inproj(x2, w_bf16, tm=512, tn=1024):
    T, K = x2.shape
    N = w_bf16.shape[1]
    return pl.pallas_call(
        _inproj_kernel,
        out_shape=jax.ShapeDtypeStruct((T, N), BF16),
        grid=(T // tm, N // tn),
        in_specs=[pl.BlockSpec((tm, K), lambda i, j: (i, 0)),
                  pl.BlockSpec((K, tn), lambda i, j: (0, j))],
        out_specs=pl.BlockSpec((tm, tn), lambda i, j: (i, j)),
        compiler_params=_vmem_limit(40),
        name="in_proj",
    )(x2, w_bf16)


def _sb_kernel(q_ref, k_ref, v_ref, m_ref, o_ref, acc_ref, r_ref, *, tq):
    i = pl.program_id(2)
    q = q_ref[...]
    lane = lax.broadcasted_iota(jnp.int32, (tq, LANES), 1)
    zero = jnp.zeros_like(q)
    q_heads = (jnp.where(lane < SB_HEAD_DIM, q, zero), jnp.where(lane >= SB_HEAD_DIM, q, zero))
    row = lax.broadcasted_iota(jnp.int32, (tq, tq), 0)
    col = lax.broadcasted_iota(jnp.int32, (tq, tq), 1)
    strict = col < row

    acc_ref[...] = jnp.zeros_like(acc_ref)
    r_ref[...] = jnp.zeros_like(r_ref)

    def visit(j, diag):
        start = pl.multiple_of(j * tq, tq)
        kb = k_ref[pl.ds(start, tq), :]
        vb = v_ref[pl.ds(start, tq), :]
        mext = m_ref[...]
        for h in range(2):
            z = lax.dot_general(q_heads[h], kb, NT_DIMS, preferred_element_type=F32)
            log_beta = jnp.minimum(z, 0.0) - jnp.log(1.0 + jnp.exp(-jnp.abs(z)))
            log_fail = log_beta - z
            if diag:
                log_fail = jnp.where(strict, log_fail, 0.0)
            hi = log_fail.astype(BF16)
            lo = (log_fail - hi.astype(F32)).astype(BF16)
            cum = (jnp.dot(hi, mext, preferred_element_type=F32)
                   + jnp.dot(lo, mext, preferred_element_type=F32))
            r_old = r_ref[h]
            arg = log_beta + cum[:, :tq] + jnp.tile(r_old, (1, tq // LANES))
            w = jnp.exp(arg)
            if diag:
                w = jnp.where(strict, w, 0.0)
            acc_ref[h] += jnp.dot(w.astype(BF16), vb, preferred_element_type=F32)
            r_ref[h] = r_old + cum[:, tq:]

    visit(i, True)

    def cond(c):
        j, live = c
        return jnp.logical_and(j >= 0, live > 0)

    def body(c):
        j, _ = c
        visit(j, False)
        r_max = jnp.max(jnp.maximum(r_ref[0], r_ref[1]))
        return j - 1, (r_max >= EXP_ZERO_BELOW).astype(jnp.int32)

    lax.while_loop(cond, body, (i - 1, jnp.int32(1)))
    o_ref[...] = jnp.where(lane < SB_HEAD_DIM, acc_ref[0], acc_ref[1]).astype(o_ref.dtype)


def _sb_attention(proj, mext, B, S, tq):
    n_pairs = SB_HEADS * SB_HEAD_DIM // LANES
    nq = S // tq
    return pl.pallas_call(
        functools.partial(_sb_kernel, tq=tq),
        out_shape=jax.ShapeDtypeStruct((B * S, n_pairs * LANES), BF16),
        grid=(B, n_pairs, nq),
        in_specs=[
            pl.BlockSpec((tq, LANES), lambda b, p, i: (b * nq + i, p)),
            pl.BlockSpec((S, LANES), lambda b, p, i: (b, n_pairs + p)),
            pl.BlockSpec((S, LANES), lambda b, p, i: (b, 2 * n_pairs + p)),
            pl.BlockSpec((tq, tq + LANES), lambda b, p, i: (0, 0)),
        ],
        out_specs=pl.BlockSpec((tq, LANES), lambda b, p, i: (b * nq + i, p)),
        scratch_shapes=[pltpu.VMEM((2, tq, LANES), F32), pltpu.VMEM((2, tq, LANES), F32)],
        compiler_params=_vmem_limit(40),
        name="sb_attn",
    )(proj, proj, proj, mext)


def _df_kernel(far_ref, lam_ref, q_ref, k_ref, v_ref, bias_ref, g_ref, o_ref, acc_ref, m_ref,
               *, tq, n_near, lam_init):
    i = pl.program_id(2)
    q = q_ref[...]
    lane = lax.broadcasted_iota(jnp.int32, (tq, LANES), 1)
    zero = jnp.zeros_like(q)
    q_maps = (jnp.where(lane < DF_HEAD_DIM, q, zero), jnp.where(lane >= DF_HEAD_DIM, q, zero))

    acc_ref[...] = jnp.zeros_like(acc_ref)
    m_ref[...] = jnp.full_like(m_ref, NEG_BIG)

    def scores(start, width):
        kb = k_ref[pl.ds(start, width), :]
        return tuple(lax.dot_general(q_maps[m], kb, NT_DIMS, preferred_element_type=F32)
                     for m in range(2))

    def update(s_maps, start, width, bias, shift):
        vb = v_ref[pl.ds(start, width), :]
        v_ext = jnp.concatenate([vb, jnp.ones((width, LANES), BF16)], axis=1)
        for m in range(2):
            s = s_maps[m]
            if bias is not None:
                s = s + bias
            m_old = m_ref[m]
            m_new = jnp.maximum(m_old, jnp.max(s, axis=-1, keepdims=True) + shift)
            p = jnp.exp(s - jnp.tile(m_new - shift, (1, width // LANES)))
            corr = jnp.exp(m_old - m_new)
            acc_ref[m] = (jnp.tile(corr, (1, 2)) * acc_ref[m]
                          + jnp.dot(p.astype(BF16), v_ext, preferred_element_type=F32))
            m_ref[m] = m_new

    def visit(start, width, bias, shift):
        update(scores(start, width), start, width, bias, shift)

    far_bias = far_ref[pl.program_id(1)]
    n_far = jnp.maximum(i - n_near + 1, 0)
    n_wide = n_far // 2
    wide = 2 * tq

    def far_body(jj, s_maps):
        nxt = jnp.minimum(jj + 1, n_wide - 1)
        s_next = scores(pl.multiple_of(nxt * wide, wide), wide)
        update(s_maps, pl.multiple_of(jj * wide, wide), wide, None, far_bias)
        return s_next

    @pl.when(n_wide > 0)
    def _():
        lax.fori_loop(0, n_wide, far_body, scores(0, wide))

    @pl.when(n_far % 2 == 1)
    def _():
        visit(pl.multiple_of((n_far - 1) * tq, tq), tq, None, far_bias)

    if n_near == 2:
        @pl.when(i >= 1)
        def _():
            bias = jnp.concatenate([bias_ref[0, 1], bias_ref[0, 0]], axis=1)
            visit(pl.multiple_of((i - 1) * tq, tq), wide, bias, 0.0)

        @pl.when(i == 0)
        def _():
            visit(0, tq, bias_ref[0, 0], 0.0)
    else:
        for d in range(n_near - 1, -1, -1):
            @pl.when(i >= d)
            def _():
                visit(pl.multiple_of((i - d) * tq, tq), tq, bias_ref[0, d], 0.0)

    lam_vec = lam_ref[...]
    lam = (jnp.exp(jnp.sum(lam_vec[0:1] * lam_vec[1:2], axis=-1, keepdims=True))
           - jnp.exp(jnp.sum(lam_vec[2:3] * lam_vec[3:4], axis=-1, keepdims=True)) + lam_init)
    o = (acc_ref[0, :, :LANES] / acc_ref[0, :, LANES:]
         - lam * (acc_ref[1, :, :LANES] / acc_ref[1, :, LANES:]))
    o = o * lax.rsqrt(jnp.mean(o * o, axis=-1, keepdims=True) + LN_EPS)
    o_ref[...] = (o * g_ref[...] * (1.0 - lam_init)).astype(o_ref.dtype)


def _df_attention(proj, lam_vec, bias_tiles, far_bias, subln_g, B, S, tq, lam_init):
    nq = S // tq
    n_near = bias_tiles.shape[1]
    q0 = 3 * SB_HEADS * SB_HEAD_DIM // LANES
    return pl.pallas_call(
        functools.partial(_df_kernel, tq=tq, n_near=n_near, lam_init=lam_init),
        out_shape=jax.ShapeDtypeStruct((B * S, DF_HEADS * LANES), BF16),
        grid=(B, DF_HEADS, nq),
        in_specs=[
            pl.BlockSpec(memory_space=pltpu.SMEM),
            pl.BlockSpec((4, DF_HEAD_DIM), lambda b, h, i: (0, 0)),
            pl.BlockSpec((tq, LANES), lambda b, h, i: (b * nq + i, q0 + h)),
            pl.BlockSpec((S, LANES), lambda b, h, i: (b, q0 + DF_HEADS + h)),
            pl.BlockSpec((S, LANES), lambda b, h, i: (b, q0 + 2 * DF_HEADS + h)),
            pl.BlockSpec((1, n_near, tq, tq), lambda b, h, i: (h, 0, 0, 0)),
            pl.BlockSpec((1, LANES), lambda b, h, i: (0, 0)),
        ],
        out_specs=pl.BlockSpec((tq, LANES), lambda b, h, i: (b * nq + i, h)),
        scratch_shapes=[pltpu.VMEM((2, tq, 2 * LANES), F32), pltpu.VMEM((2, tq, LANES), F32)],
        compiler_params=_vmem_limit(40),
        name="df_attn",
    )(far_bias, lam_vec, proj, proj, proj, bias_tiles, subln_g)


def _t5_bucket(rel):
    half = N_BUCKETS // 2
    max_exact = half // 2
    ret = (rel > 0).astype(jnp.int32) * half
    n = jnp.abs(rel)
    nf = jnp.maximum(n, 1).astype(F32)
    large = max_exact + (jnp.log(nf / max_exact) / math.log(MAX_DISTANCE / max_exact)
                         * (half - max_exact)).astype(jnp.int32)
    large = jnp.minimum(large, half - 1)
    return ret + jnp.where(n < max_exact, n, large)


def _df_bias_tiles(rel_bias, tq):
    n_near = -(-(MAX_DISTANCE - 1) // tq) + 1
    r = jnp.arange(tq, dtype=jnp.int32)[:, None]
    c = jnp.arange(tq, dtype=jnp.int32)[None, :]
    tiles = []
    for d in range(n_near):
        b = rel_bias[_t5_bucket(c - r - d * tq)].astype(F32)
        if d == 0:
            allowed = (c // CHUNK) <= (r // CHUNK)
            b = jnp.where(allowed[:, :, None], b, NEG_BIG)
        tiles.append(b.transpose(2, 0, 1))
    far = rel_bias[_t5_bucket(jnp.int32(-MAX_DISTANCE))].astype(F32)
    return jnp.stack(tiles, axis=1), far


def _layer_norm(y, g, b):
    mu = jnp.mean(y, axis=-1, keepdims=True)
    d = y - mu
    var = jnp.mean(d * d, axis=-1, keepdims=True)
    return d * lax.rsqrt(var + LN_EPS) * g + b


def _merge_kernel(osb_ref, odf_ref, ga_ref, gb_ref, x_ref, wa_ref, wb_ref, wo_ref, g_ref, b_ref,
                  h_ref, *, alpha):
    a = jnp.dot(osb_ref[...], wa_ref[...], preferred_element_type=F32)
    b = jnp.dot(odf_ref[...], wb_ref[...], preferred_element_type=F32)
    merged = (jax.nn.sigmoid(ga_ref[...].astype(F32)) * a
              + jax.nn.sigmoid(gb_ref[...].astype(F32)) * b)
    mix = jnp.dot(merged.astype(BF16), wo_ref[...], preferred_element_type=F32)
    h_ref[...] = _layer_norm(alpha * x_ref[...] + mix, g_ref[...], b_ref[...])


def _merge(o_sb, o_df, proj, x2, w_a, w_b, w_o, ln_g, ln_b, alpha, tm=512):
    T, D = x2.shape
    Wa, Wb = o_sb.shape[1], o_df.shape[1]
    gate_blk = (proj.shape[1] - 2 * D) // D
    const = lambda i: (0, 0)
    return pl.pallas_call(
        functools.partial(_merge_kernel, alpha=alpha),
        out_shape=jax.ShapeDtypeStruct((T, D), F32),
        grid=(T // tm,),
        in_specs=[
            pl.BlockSpec((tm, Wa), lambda i: (i, 0)),
            pl.BlockSpec((tm, Wb), lambda i: (i, 0)),
            pl.BlockSpec((tm, D), lambda i: (i, gate_blk)),
            pl.BlockSpec((tm, D), lambda i: (i, gate_blk + 1)),
            pl.BlockSpec((tm, D), lambda i: (i, 0)),
            pl.BlockSpec((Wa, D), const),
            pl.BlockSpec((Wb, D), const),
            pl.BlockSpec((D, D), const),
            pl.BlockSpec((1, D), const),
            pl.BlockSpec((1, D), const),
        ],
        out_specs=pl.BlockSpec((tm, D), lambda i: (i, 0)),
        compiler_params=_vmem_limit(40),
        name="merge_ln1",
    )(o_sb, o_df, proj, proj, x2, w_a, w_b, w_o, ln_g, ln_b)


def _top16(s, ids):
    id_bound = jnp.iinfo(jnp.int32).max
    vals, idxs = [], []
    for _ in range(PEER_TOPK):
        m = jnp.max(s, axis=0, keepdims=True)
        idx = jnp.min(jnp.where(s == m, ids, id_bound), axis=0, keepdims=True)
        vals.append(m)
        idxs.append(idx)
        s = jnp.where(ids == idx, -jnp.inf, s)
    return jnp.concatenate(vals, axis=0), jnp.concatenate(idxs, axis=0)


def _pair_candidates(v1, v2, tm):
    sub = lax.broadcasted_iota(jnp.int32, (SUBLANES, tm), 0)
    k = PEER_TOPK
    vals = [v1[0:1] + v2[0:SUBLANES], v1[0:1] + v2[SUBLANES:k]]
    ids = [sub, sub + SUBLANES]
    for a in range(1, SUBLANES):
        vals.append(v1[a:a + 1] + v2[0:SUBLANES])
        ids.append(sub + a * k)
    vals.append(v1[SUBLANES:k] + v2[0:1])
    ids.append((sub + SUBLANES) * k)
    return jnp.concatenate(vals, axis=0), jnp.concatenate(ids, axis=0)


def _pick_rows(table, sel):
    out = jnp.zeros_like(table)
    for a in range(PEER_TOPK):
        out = jnp.where(sel == a, table[a:a + 1, :], out)
    return out


def _peer_select_kernel(h_ref, wq_ref, k1_ref, k2_ref, e_ref, g_ref, *, tm, row_words):
    q = jnp.dot(h_ref[...].astype(BF16), wq_ref[...], preferred_element_type=F32)
    k1 = k1_ref[...]
    k2 = k2_ref[...]
    half = k1.shape[1]
    key_ids = lax.broadcasted_iota(jnp.int32, (PEER_N_KEYS, tm), 0)
    for hd in range(PEER_HEADS):
        q1 = q[:, (2 * hd) * half:(2 * hd + 1) * half].astype(BF16)
        q2 = q[:, (2 * hd + 1) * half:(2 * hd + 2) * half].astype(BF16)
        s1 = lax.dot_general(k1, q1, NT_DIMS, preferred_element_type=F32)
        s2 = lax.dot_general(k2, q2, NT_DIMS, preferred_element_type=F32)
        v1, i1 = _top16(s1, key_ids)
        v2, i2 = _top16(s2, key_ids)
        sc, ci = _top16(*_pair_candidates(v1, v2, tm))
        e = (_pick_rows(i1, ci // PEER_TOPK) * PEER_N_KEYS + _pick_rows(i2, ci % PEER_TOPK))
        p = jnp.exp(sc - sc[0:1, :])
        g = p / jnp.sum(p, axis=0, keepdims=True)
        e_ref[hd * PEER_TOPK:(hd + 1) * PEER_TOPK, :] = e * row_words + SUBLANES
        g_ref[hd * PEER_TOPK:(hd + 1) * PEER_TOPK, :] = g


def _peer_select(h2, wq, k1, k2, row_words, tm=256):
    T, D = h2.shape
    n_slots = PEER_HEADS * PEER_TOPK
    const = lambda i: (0, 0)
    return pl.pallas_call(
        functools.partial(_peer_select_kernel, tm=tm, row_words=row_words),
        out_shape=(jax.ShapeDtypeStruct((n_slots, T), jnp.int32),
                   jax.ShapeDtypeStruct((n_slots, T), F32)),
        grid=(T // tm,),
        in_specs=[pl.BlockSpec((tm, D), lambda i: (i, 0)),
                  pl.BlockSpec(wq.shape, const),
                  pl.BlockSpec(k1.shape, const),
                  pl.BlockSpec(k2.shape, const)],
        out_specs=(pl.BlockSpec((n_slots, tm), lambda i: (0, i)),
                   pl.BlockSpec((n_slots, tm), lambda i: (0, i))),
        compiler_params=_vmem_limit(40),
        name="peer_select",
    )(h2, wq, k1, k2)


def _load_table(tbl_hbm, tbl, sem):
    n_words = tbl_hbm.shape[0]

    @pl.when(pl.program_id(0) == 0)
    def _():
        pad = jnp.zeros((SUBLANES, LANES), tbl.dtype)
        tbl[pl.ds(0, SUBLANES), :] = pad
        tbl[pl.ds(SUBLANES + n_words, SUBLANES), :] = pad
        cp = pltpu.make_async_copy(tbl_hbm, tbl.at[pl.ds(SUBLANES, n_words)], sem)
        cp.start()
        cp.wait()


def _pair_words(tbl, e_ref, p, t, lo_half):
    n_pairs = e_ref.shape[1] // 2
    va = tbl[pl.ds(e_ref[t, p], SUBLANES), :]
    vb = tbl[pl.ds(e_ref[t, p + n_pairs] - HALF, SUBLANES), :]
    return jnp.where(lo_half, va, vb)


def _pair_rows(tbl, e_ref, p, t, lo_half):
    pr = _pair_words(tbl, e_ref, p, t, lo_half)
    lo = lax.bitcast_convert_type(pr << 16, F32)
    hi = lax.bitcast_convert_type(pr & jnp.uint32(0xFFFF0000), F32)
    return lo, hi


def _peer_u_kernel(e_ref, x_ref, g_ref, tbl_hbm, w_ref, tbl, sem, *, tb):
    _load_table(tbl_hbm, tbl, sem)
    n_pairs = e_ref.shape[1] // 2
    sub = lax.broadcasted_iota(jnp.int32, (SUBLANES, LANES), 0)
    lane = lax.broadcasted_iota(jnp.int32, (SUBLANES, LANES), 1)
    lo_half = sub < HALF
    ones = jnp.ones((LANES, LANES), BF16)

    def token(t):
        xt = x_ref[t]
        xr = pltpu.roll(xt, HALF, axis=0)
        x_lo = jnp.where(lo_half, xt, xr)
        x_hi = jnp.where(lo_half, xr, xt)
        tile = jnp.zeros((SUBLANES, LANES), F32)
        for p0 in range(0, n_pairs, PAIRS_PER_DOT):
            prods = []
            for p in range(p0, p0 + PAIRS_PER_DOT):
                lo, hi = _pair_rows(tbl, e_ref, p, t, lo_half)
                prods.append(lo * x_lo + hi * x_hi)
            sums = jnp.dot(jnp.concatenate(prods, axis=0).astype(BF16), ones,
                           preferred_element_type=F32)
            for k in range(PAIRS_PER_DOT):
                tile = jnp.where(lane == p0 + k, sums[k * SUBLANES:(k + 1) * SUBLANES], tile)
        tile = tile + pltpu.roll(tile, SUBLANES - 1, axis=0)
        tile = tile + pltpu.roll(tile, SUBLANES - 2, axis=0)
        w_ref[pl.ds(t, 1), :] = jnp.where(lane[0:1] < n_pairs, tile[0:1],
                                          pltpu.roll(tile, n_pairs, axis=1)[HALF:HALF + 1])

    for t in range(tb):
        token(t)
    w_ref[...] = g_ref[...] * jax.nn.gelu(w_ref[...])


def _peer_v_kernel(e_ref, w_ref, h_ref, tbl_hbm, g_ref, b_ref, o_ref, tbl, sem, *, tb, alpha):
    _load_table(tbl_hbm, tbl, sem)
    n_pairs = e_ref.shape[1] // 2
    n_rows = w_ref.shape[1]
    sub = lax.broadcasted_iota(jnp.int32, (SUBLANES, LANES), 0)
    lo_half = sub < HALF
    n_feat = SUBLANES * LANES
    r = lax.broadcasted_iota(jnp.int32, (SUBLANES, n_rows), 1)
    own_chunk = (HALF * (r % 2) + (r // 2) % HALF
                 == lax.broadcasted_iota(jnp.int32, (SUBLANES, n_rows), 0))

    def token(t):
        rows = [pltpu.bitcast(_pair_words(tbl, e_ref, p, t, lo_half), BF16) for p in range(n_pairs)]
        stacked = jnp.concatenate(rows, axis=0)
        w_row = jnp.broadcast_to(w_ref[pl.ds(t, 1), :], (SUBLANES, n_rows))
        w_mat = jnp.where(own_chunk, w_row, 0.0).astype(BF16)
        o_ref[t] = jnp.dot(w_mat, stacked, preferred_element_type=F32)

    for t in range(tb):
        token(t)

    y = alpha * h_ref[...] + o_ref[...]
    mu = jnp.sum(jnp.sum(y, axis=2, keepdims=True), axis=1, keepdims=True) / n_feat
    d = y - mu
    var = jnp.sum(jnp.sum(d * d, axis=2, keepdims=True), axis=1, keepdims=True) / n_feat
    o_ref[...] = d * lax.rsqrt(var + LN_EPS) * g_ref[...] + b_ref[...]


def _pack_table(tab):
    E, D = tab.shape
    bits = lax.bitcast_convert_type(tab.astype(BF16), jnp.uint16).astype(jnp.uint32)
    packed = bits[:, :D // 2] | (bits[:, D // 2:] << 16)
    return packed.reshape(E * (D // 2) // LANES, LANES)


def _peer_scratch(tbl_words):
    return [pltpu.VMEM((tbl_words.shape[0] + 2 * SUBLANES, LANES), jnp.uint32), pltpu.SemaphoreType.DMA]


def _peer_u(e_tok, h_tiles, g_tok, tbl_words, tb=PEER_TOKEN_BLOCK):
    T, n_slots = e_tok.shape
    return pl.pallas_call(
        functools.partial(_peer_u_kernel, tb=tb),
        out_shape=jax.ShapeDtypeStruct((T, n_slots), F32),
        grid=(T // tb,),
        in_specs=[
            pl.BlockSpec((tb, n_slots), lambda i: (i, 0), memory_space=pltpu.SMEM),
            pl.BlockSpec((tb, SUBLANES, LANES), lambda i: (i, 0, 0)),
            pl.BlockSpec((tb, n_slots), lambda i: (i, 0)),
            pl.BlockSpec(memory_space=pl.ANY),
        ],
        out_specs=pl.BlockSpec((tb, n_slots), lambda i: (i, 0)),
        scratch_shapes=_peer_scratch(tbl_words),
        compiler_params=_vmem_limit(48),
        name="peer_u",
    )(e_tok, h_tiles, g_tok, tbl_words)


def _peer_v(e_tok, w_rows, h_tiles, tbl_words, ln_g, ln_b, alpha, tb=PEER_TOKEN_BLOCK):
    T, n_slots = e_tok.shape
    n_rows = w_rows.shape[1]
    const = lambda i: (0, 0)
    return pl.pallas_call(
        functools.partial(_peer_v_kernel, tb=tb, alpha=alpha),
        out_shape=jax.ShapeDtypeStruct((T, SUBLANES, LANES), F32),
        grid=(T // tb,),
        in_specs=[
            pl.BlockSpec((tb, n_slots), lambda i: (i, 0), memory_space=pltpu.SMEM),
            pl.BlockSpec((tb, n_rows), lambda i: (i, 0)),
            pl.BlockSpec((tb, SUBLANES, LANES), lambda i: (i, 0, 0)),
            pl.BlockSpec(memory_space=pl.ANY),
            pl.BlockSpec((SUBLANES, LANES), const),
            pl.BlockSpec((SUBLANES, LANES), const),
        ],
        out_specs=pl.BlockSpec((tb, SUBLANES, LANES), lambda i: (i, 0, 0)),
        scratch_shapes=_peer_scratch(tbl_words),
        compiler_params=_vmem_limit(48),
        name="peer_v",
    )(e_tok, w_rows, h_tiles, tbl_words, ln_g, ln_b)


def kernel(x, w_in, lambda_q1, lambda_k1, lambda_q2, lambda_k2, diff_subln_g, rel_bias, w_branch_a, w_branch_b, w_out, ln1_g, ln1_b, peer_wq, peer_k1, peer_k2, peer_u, peer_v, ln2_g, ln2_b):
    B, S, D = x.shape
    T = B * S
    depth = w_in.shape[0]
    alpha = (2.0 * depth) ** 0.25
    tq = 256
    assert D == SUBLANES * LANES and S % tq == 0 and tq % CHUNK == 0

    sb_w = SB_HEADS * SB_HEAD_DIM
    df_w = DF_HEADS * 2 * DF_HEAD_DIM
    assert SB_HEAD_DIM == 64 and DF_HEAD_DIM == 64
    col_scale = jnp.ones((w_in.shape[2],), F32)
    col_scale = col_scale.at[:sb_w].set(0.125).at[3 * sb_w:3 * sb_w + df_w].set(0.125)

    r = jnp.arange(tq, dtype=jnp.int32)
    later = (r[:, None] > r[None, :]).astype(BF16)
    mext = jnp.concatenate([later, jnp.ones((tq, LANES), BF16)], axis=1)
    bias_tiles, far_bias = _df_bias_tiles(rel_bias, tq)
    row_words = D // 2 // LANES
    n_slots = PEER_HEADS * PEER_TOPK

    h = x.reshape(T, D)
    for l in range(depth):
        lam_init = 0.8 - 0.6 * math.exp(-0.3 * l)
        proj = _inproj(h, (w_in[l] * col_scale).astype(BF16))
        o_sb = _sb_attention(proj, mext, B, S, tq)
        lam_vec = jnp.stack([lambda_q1[l], lambda_k1[l], lambda_q2[l], lambda_k2[l]]).astype(F32)
        o_df = _df_attention(proj, lam_vec, bias_tiles, far_bias, diff_subln_g[l].reshape(1, -1).astype(F32),
                             B, S, tq, lam_init)
        h = _merge(o_sb, o_df, proj, h, w_branch_a[l].astype(BF16), w_branch_b[l].astype(BF16),
                   w_out[l].astype(BF16), ln1_g[l].reshape(1, D), ln1_b[l].reshape(1, D), alpha)
        e_t, g_t = _peer_select(h, peer_wq[l].astype(BF16), peer_k1[l].astype(BF16),
                                peer_k2[l].astype(BF16), row_words)
        e_tok = e_t.T
        h_tiles = h.reshape(T, SUBLANES, LANES)
        w_tok = _peer_u(e_tok, h_tiles, g_t.T, _pack_table(peer_u[l]))
        w_rows = jnp.repeat(w_tok.reshape(T, 2, n_slots // 2).swapaxes(1, 2).reshape(T, n_slots),
                            SUBLANES, axis=1)
        h = _peer_v(e_tok, w_rows, h_tiles, _pack_table(peer_v[l]), ln2_g[l].reshape(SUBLANES, LANES),
                    ln2_b[l].reshape(SUBLANES, LANES), alpha).reshape(T, D)
    return h.reshape(B, S, D)
```

```python
import functools
import math

import jax
import jax.numpy as jnp
from jax import lax
from jax.experimental import pallas as pl
from jax.experimental.pallas import tpu as pltpu

F32 = jnp.float32
BF16 = jnp.bfloat16

CHUNK = 64
SB_HEADS = 8
SB_HEAD_DIM = 64
DF_HEADS = 4
DF_HEAD_DIM = 64
N_BUCKETS = 32
MAX_DISTANCE = 256
PEER_HEADS = 8
PEER_N_KEYS = 128
PEER_TOPK = 16
LN_EPS = 1e-5

LANES = 128
SUBLANES = 8
HALF = SUBLANES // 2
PEER_TOKEN_BLOCK = 128
PAIRS_PER_DOT = 16
NEG_BIG = -1e30
EXP_ZERO_BELOW = -104.0

NT_DIMS = (((1,), (1,)), ((), ()))


def _vmem_limit(mib):
    return pltpu.CompilerParams(vmem_limit_bytes=mib * 1024 * 1024)


def _inproj_kernel(x_ref, w_ref, o_ref):
    o_ref[...] = jnp.dot(x_ref[...].astype(BF16), w_ref[...],
                         preferred_element_type=F32).astype(o_ref.dtype)


def _inproj(x2, w_bf16, tm=512, tn=1024):
    T, K = x2.shape
    N = w_bf16.shape[1]
    return pl.pallas_call(
        _inproj_kernel,
        out_shape=jax.ShapeDtypeStruct((T, N), BF16),
        grid=(T // tm, N // tn),
        in_specs=[pl.BlockSpec((tm, K), lambda i, j: (i, 0)),
                  pl.BlockSpec((K, tn), lambda i, j: (0, j))],
        out_specs=pl.BlockSpec((tm, tn), lambda i, j: (i, j)),
        compiler_params=_vmem_limit(40),
        name="in_proj",
    )(x2, w_bf16)


def _sb_kernel(q_ref, k_ref, v_ref, m_ref, o_ref, acc_ref, r_ref, *, tq):
    i = pl.program_id(2)
    q = q_ref[...]
    lane = lax.broadcasted_iota(jnp.int32, (tq, LANES), 1)
    zero = jnp.zeros_like(q)
    q_heads = (jnp.where(lane < SB_HEAD_DIM, q, zero), jnp.where(lane >= SB_HEAD_DIM, q, zero))
    row = lax.broadcasted_iota(jnp.int32, (tq, tq), 0)
    col = lax.broadcasted_iota(jnp.int32, (tq, tq), 1)
    strict = col < row

    acc_ref[...] = jnp.zeros_like(acc_ref)
    r_ref[...] = jnp.zeros_like(r_ref)

    def visit(j, diag):
        start = pl.multiple_of(j * tq, tq)
        kb = k_ref[pl.ds(start, tq), :]
        vb = v_ref[pl.ds(start, tq), :]
        mext = m_ref[...]
        for h in range(2):
            z = lax.dot_general(q_heads[h], kb, NT_DIMS, preferred_element_type=F32)
            log_beta = jnp.minimum(z, 0.0) - jnp.log(1.0 + jnp.exp(-jnp.abs(z)))
            log_fail = log_beta - z
            if diag:
                log_fail = jnp.where(strict, log_fail, 0.0)
            hi = log_fail.astype(BF16)
            lo = (log_fail - hi.astype(F32)).astype(BF16)
            cum = (jnp.dot(hi, mext, preferred_element_type=F32)
                   + jnp.dot(lo, mext, preferred_element_type=F32))
            r_old = r_ref[h]
            arg = log_beta + cum[:, :tq] + jnp.tile(r_old, (1, tq // LANES))
            w = jnp.exp(arg)
            if diag:
                w = jnp.where(strict, w, 0.0)
            acc_ref[h] += jnp.dot(w.astype(BF16), vb, preferred_element_type=F32)
            r_ref[h] = r_old + cum[:, tq:]

    visit(i, True)

    def cond(c):
        j, live = c
        return jnp.logical_and(j >= 0, live > 0)

    def body(c):
        j, _ = c
        visit(j, False)
        r_max = jnp.max(jnp.maximum(r_ref[0], r_ref[1]))
        return j - 1, (r_max >= EXP_ZERO_BELOW).astype(jnp.int32)

    lax.while_loop(cond, body, (i - 1, jnp.int32(1)))
    o_ref[...] = jnp.where(lane < SB_HEAD_DIM, acc_ref[0], acc_ref[1]).astype(o_ref.dtype)


def _sb_attention(proj, mext, B, S, tq):
    n_pairs = SB_HEADS * SB_HEAD_DIM // LANES
    nq = S // tq
    return pl.pallas_call(
        functools.partial(_sb_kernel, tq=tq),
        out_shape=jax.ShapeDtypeStruct((B * S, n_pairs * LANES), BF16),
        grid=(B, n_pairs, nq),
        in_specs=[
            pl.BlockSpec((tq, LANES), lambda b, p, i: (b * nq + i, p)),
            pl.BlockSpec((S, LANES), lambda b, p, i: (b, n_pairs + p)),
            pl.BlockSpec((S, LANES), lambda b, p, i: (b, 2 * n_pairs + p)),
            pl.BlockSpec((tq, tq + LANES), lambda b, p, i: (0, 0)),
        ],
        out_specs=pl.BlockSpec((tq, LANES), lambda b, p, i: (b * nq + i, p)),
        scratch_shapes=[pltpu.VMEM((2, tq, LANES), F32), pltpu.VMEM((2, tq, LANES), F32)],
        compiler_params=_vmem_limit(40),
        name="sb_attn",
    )(proj, proj, proj, mext)


def _df_kernel(far_ref, lam_ref, q_ref, k_ref, v_ref, bias_ref, g_ref, o_ref, acc_ref, m_ref,
               *, tq, n_near, lam_init):
    i = pl.program_id(2)
    q = q_ref[...]
    lane = lax.broadcasted_iota(jnp.int32, (tq, LANES), 1)
    zero = jnp.zeros_like(q)
    q_maps = (jnp.where(lane < DF_HEAD_DIM, q, zero), jnp.where(lane >= DF_HEAD_DIM, q, zero))

    acc_ref[...] = jnp.zeros_like(acc_ref)
    m_ref[...] = jnp.full_like(m_ref, NEG_BIG)

    def scores(start, width):
        kb = k_ref[pl.ds(start, width), :]
        return tuple(lax.dot_general(q_maps[m], kb, NT_DIMS, preferred_element_type=F32)
                     for m in range(2))

    def update(s_maps, start, width, bias, shift):
        vb = v_ref[pl.ds(start, width), :]
        v_ext = jnp.concatenate([vb, jnp.ones((width, LANES), BF16)], axis=1)
        for m in range(2):
            s = s_maps[m]
            if bias is not None:
                s = s + bias
            m_old = m_ref[m]
            m_new = jnp.maximum(m_old, jnp.max(s, axis=-1, keepdims=True) + shift)
            p = jnp.exp(s - jnp.tile(m_new - shift, (1, width // LANES)))
            corr = jnp.exp(m_old - m_new)
            acc_ref[m] = (jnp.tile(corr, (1, 2)) * acc_ref[m]
                          + jnp.dot(p.astype(BF16), v_ext, preferred_element_type=F32))
            m_ref[m] = m_new

    def visit(start, width, bias, shift):
        update(scores(start, width), start, width, bias, shift)

    far_bias = far_ref[pl.program_id(1)]
    n_far = jnp.maximum(i - n_near + 1, 0)
    n_wide = n_far // 2
    wide = 2 * tq

    def far_body(jj, s_maps):
        nxt = jnp.minimum(jj + 1, n_wide - 1)
        s_next = scores(pl.multiple_of(nxt * wide, wide), wide)
        update(s_maps, pl.multiple_of(jj * wide, wide), wide, None, far_bias)
        return s_next

    @pl.when(n_wide > 0)
    def _():
        lax.fori_loop(0, n_wide, far_body, scores(0, wide))

    @pl.when(n_far % 2 == 1)
    def _():
        visit(pl.multiple_of((n_far - 1) * tq, tq), tq, None, far_bias)

    if n_near == 2:
        @pl.when(i >= 1)
        def _():
            bias = jnp.concatenate([bias_ref[0, 1], bias_ref[0, 0]], axis=1)
            visit(pl.multiple_of((i - 1) * tq, tq), wide, bias, 0.0)

        @pl.when(i == 0)
        def _():
            visit(0, tq, bias_ref[0, 0], 0.0)
    else:
        for d in range(n_near - 1, -1, -1):
            @pl.when(i >= d)
            def _():
                visit(pl.multiple_of((i - d) * tq, tq), tq, bias_ref[0, d], 0.0)

    lam_vec = lam_ref[...]
    lam = (jnp.exp(jnp.sum(lam_vec[0:1] * lam_vec[1:2], axis=-1, keepdims=True))
           - jnp.exp(jnp.sum(lam_vec[2:3] * lam_vec[3:4], axis=-1, keepdims=True)) + lam_init)
    o = (acc_ref[0, :, :LANES] / acc_ref[0, :, LANES:]
         - lam * (acc_ref[1, :, :LANES] / acc_ref[1, :, LANES:]))
    o = o * lax.rsqrt(jnp.mean(o * o, axis=-1, keepdims=True) + LN_EPS)
    o_ref[...] = (o * g_ref[...] * (1.0 - lam_init)).astype(o_ref.dtype)


def _df_attention(proj, lam_vec, bias_tiles, far_bias, subln_g, B, S, tq, lam_init):
    nq = S // tq
    n_near = bias_tiles.shape[1]
    q0 = 3 * SB_HEADS * SB_HEAD_DIM // LANES
    return pl.pallas_call(
        functools.partial(_df_kernel, tq=tq, n_near=n_near, lam_init=lam_init),
        out_shape=jax.ShapeDtypeStruct((B * S, DF_HEADS * LANES), BF16),
        grid=(B, DF_HEADS, nq),
        in_specs=[
            pl.BlockSpec(memory_space=pltpu.SMEM),
            pl.BlockSpec((4, DF_HEAD_DIM), lambda b, h, i: (0, 0)),
            pl.BlockSpec((tq, LANES), lambda b, h, i: (b * nq + i, q0 + h)),
            pl.BlockSpec((S, LANES), lambda b, h, i: (b, q0 + DF_HEADS + h)),
            pl.BlockSpec((S, LANES), lambda b, h, i: (b, q0 + 2 * DF_HEADS + h)),
            pl.BlockSpec((1, n_near, tq, tq), lambda b, h, i: (h, 0, 0, 0)),
            pl.BlockSpec((1, LANES), lambda b, h, i: (0, 0)),
        ],
        out_specs=pl.BlockSpec((tq, LANES), lambda b, h, i: (b * nq + i, h)),
        scratch_shapes=[pltpu.VMEM((2, tq, 2 * LANES), F32), pltpu.VMEM((2, tq, LANES), F32)],
        compiler_params=_vmem_limit(40),
        name="df_attn",
    )(far_bias, lam_vec, proj, proj, proj, bias_tiles, subln_g)


def _t5_bucket(rel):
    half = N_BUCKETS // 2
    max_exact = half // 2
    ret = (rel > 0).astype(jnp.int32) * half
    n = jnp.abs(rel)
    nf = jnp.maximum(n, 1).astype(F32)
    large = max_exact + (jnp.log(nf / max_exact) / math.log(MAX_DISTANCE / max_exact)
                         * (half - max_exact)).astype(jnp.int32)
    large = jnp.minimum(large, half - 1)
    return ret + jnp.where(n < max_exact, n, large)


def _df_bias_tiles(rel_bias, tq):
    n_near = -(-(MAX_DISTANCE - 1) // tq) + 1
    r = jnp.arange(tq, dtype=jnp.int32)[:, None]
    c = jnp.arange(tq, dtype=jnp.int32)[None, :]
    tiles = []
    for d in range(n_near):
        b = rel_bias[_t5_bucket(c - r - d * tq)].astype(F32)
        if d == 0:
            allowed = (c // CHUNK) <= (r // CHUNK)
            b = jnp.where(allowed[:, :, None], b, NEG_BIG)
        tiles.append(b.transpose(2, 0, 1))
    far = rel_bias[_t5_bucket(jnp.int32(-MAX_DISTANCE))].astype(F32)
    return jnp.stack(tiles, axis=1), far


def _layer_norm(y, g, b):
    mu = jnp.mean(y, axis=-1, keepdims=True)
    d = y - mu
    var = jnp.mean(d * d, axis=-1, keepdims=True)
    return d * lax.rsqrt(var + LN_EPS) * g + b


def _merge_kernel(osb_ref, odf_ref, ga_ref, gb_ref, x_ref, wa_ref, wb_ref, wo_ref, g_ref, b_ref,
                  h_ref, *, alpha):
    a = jnp.dot(osb_ref[...], wa_ref[...], preferred_element_type=F32)
    b = jnp.dot(odf_ref[...], wb_ref[...], preferred_element_type=F32)
    merged = (jax.nn.sigmoid(ga_ref[...].astype(F32)) * a
              + jax.nn.sigmoid(gb_ref[...].astype(F32)) * b)
    mix = jnp.dot(merged.astype(BF16), wo_ref[...], preferred_element_type=F32)
    h_ref[...] = _layer_norm(alpha * x_ref[...] + mix, g_ref[...], b_ref[...])


def _merge(o_sb, o_df, proj, x2, w_a, w_b, w_o, ln_g, ln_b, alpha, tm=512):
    T, D = x2.shape
    Wa, Wb = o_sb.shape[1], o_df.shape[1]
    gate_blk = (proj.shape[1] - 2 * D) // D
    const = lambda i: (0, 0)
    return pl.pallas_call(
        functools.partial(_merge_kernel, alpha=alpha),
        out_shape=jax.ShapeDtypeStruct((T, D), F32),
        grid=(T // tm,),
        in_specs=[
            pl.BlockSpec((tm, Wa), lambda i: (i, 0)),
            pl.BlockSpec((tm, Wb), lambda i: (i, 0)),
            pl.BlockSpec((tm, D), lambda i: (i, gate_blk)),
            pl.BlockSpec((tm, D), lambda i: (i, gate_blk + 1)),
            pl.BlockSpec((tm, D), lambda i: (i, 0)),
            pl.BlockSpec((Wa, D), const),
            pl.BlockSpec((Wb, D), const),
            pl.BlockSpec((D, D), const),
            pl.BlockSpec((1, D), const),
            pl.BlockSpec((1, D), const),
        ],
        out_specs=pl.BlockSpec((tm, D), lambda i: (i, 0)),
        compiler_params=_vmem_limit(40),
        name="merge_ln1",
    )(o_sb, o_df, proj, proj, x2, w_a, w_b, w_o, ln_g, ln_b)


def _top16(s, ids):
    id_bound = jnp.iinfo(jnp.int32).max
    vals, idxs = [], []
    for _ in range(PEER_TOPK):
        m = jnp.max(s, axis=0, keepdims=True)
        idx = jnp.min(jnp.where(s == m, ids, id_bound), axis=0, keepdims=True)
        vals.append(m)
        idxs.append(idx)
        s = jnp.where(ids == idx, -jnp.inf, s)
    return jnp.concatenate(vals, axis=0), jnp.concatenate(idxs, axis=0)


def _pair_candidates(v1, v2, tm):
    sub = lax.broadcasted_iota(jnp.int32, (SUBLANES, tm), 0)
    k = PEER_TOPK
    vals = [v1[0:1] + v2[0:SUBLANES], v1[0:1] + v2[SUBLANES:k]]
    ids = [sub, sub + SUBLANES]
    for a in range(1, SUBLANES):
        vals.append(v1[a:a + 1] + v2[0:SUBLANES])
        ids.append(sub + a * k)
    vals.append(v1[SUBLANES:k] + v2[0:1])
    ids.append((sub + SUBLANES) * k)
    return jnp.concatenate(vals, axis=0), jnp.concatenate(ids, axis=0)


def _pick_rows(table, sel):
    out = jnp.zeros_like(table)
    for a in range(PEER_TOPK):
        out = jnp.where(sel == a, table[a:a + 1, :], out)
    return out


def _peer_select_kernel(h_ref, wq_ref, k1_ref, k2_ref, e_ref, g_ref, *, tm, row_words):
    q = jnp.dot(h_ref[...].astype(BF16), wq_ref[...], preferred_element_type=F32)
    k1 = k1_ref[...]
    k2 = k2_ref[...]
    half = k1.shape[1]
    key_ids = lax.broadcasted_iota(jnp.int32, (PEER_N_KEYS, tm), 0)
    for hd in range(PEER_HEADS):
        q1 = q[:, (2 * hd) * half:(2 * hd + 1) * half].astype(BF16)
        q2 = q[:, (2 * hd + 1) * half:(2 * hd + 2) * half].astype(BF16)
        s1 = lax.dot_general(k1, q1, NT_DIMS, preferred_element_type=F32)
        s2 = lax.dot_general(k2, q2, NT_DIMS, preferred_element_type=F32)
        v1, i1 = _top16(s1, key_ids)
        v2, i2 = _top16(s2, key_ids)
        sc, ci = _top16(*_pair_candidates(v1, v2, tm))
        e = (_pick_rows(i1, ci // PEER_TOPK) * PEER_N_KEYS + _pick_rows(i2, ci % PEER_TOPK))
        p = jnp.exp(sc - sc[0:1, :])
        g = p / jnp.sum(p, axis=0, keepdims=True)
        e_ref[hd * PEER_TOPK:(hd + 1) * PEER_TOPK, :] = e * row_words + SUBLANES
        g_ref[hd * PEER_TOPK:(hd + 1) * PEER_TOPK, :] = g


def _peer_select(h2, wq, k1, k2, row_words, tm=256):
    T, D = h2.shape
    n_slots = PEER_HEADS * PEER_TOPK
    const = lambda i: (0, 0)
    return pl.pallas_call(
        functools.partial(_peer_select_kernel, tm=tm, row_words=row_words),
        out_shape=(jax.ShapeDtypeStruct((n_slots, T), jnp.int32),
                   jax.ShapeDtypeStruct((n_slots, T), F32)),
        grid=(T // tm,),
        in_specs=[pl.BlockSpec((tm, D), lambda i: (i, 0)),
                  pl.BlockSpec(wq.shape, const),
                  pl.BlockSpec(k1.shape, const),
                  pl.BlockSpec(k2.shape, const)],
        out_specs=(pl.BlockSpec((n_slots, tm), lambda i: (0, i)),
                   pl.BlockSpec((n_slots, tm), lambda i: (0, i))),
        compiler_params=_vmem_limit(40),
        name="peer_select",
    )(h2, wq, k1, k2)


def _load_table(tbl_hbm, tbl, sem):
    n_words = tbl_hbm.shape[0]

    @pl.when(pl.program_id(0) == 0)
    def _():
        pad = jnp.zeros((SUBLANES, LANES), tbl.dtype)
        tbl[pl.ds(0, SUBLANES), :] = pad
        tbl[pl.ds(SUBLANES + n_words, SUBLANES), :] = pad
        cp = pltpu.make_async_copy(tbl_hbm, tbl.at[pl.ds(SUBLANES, n_words)], sem)
        cp.start()
        cp.wait()


def _pair_words(tbl, e_ref, p, t, lo_half):
    n_pairs = e_ref.shape[1] // 2
    va = tbl[pl.ds(e_ref[t, p], SUBLANES), :]
    vb = tbl[pl.ds(e_ref[t, p + n_pairs] - HALF, SUBLANES), :]
    return jnp.where(lo_half, va, vb)


def _pair_rows(tbl, e_ref, p, t, lo_half):
    pr = _pair_words(tbl, e_ref, p, t, lo_half)
    lo = lax.bitcast_convert_type(pr << 16, F32)
    hi = lax.bitcast_convert_type(pr & jnp.uint32(0xFFFF0000), F32)
    return lo, hi


def _peer_u_kernel(e_ref, x_ref, g_ref, tbl_hbm, w_ref, tbl, sem, *, tb):
    _load_table(tbl_hbm, tbl, sem)
    n_pairs = e_ref.shape[1] // 2
    sub = lax.broadcasted_iota(jnp.int32, (SUBLANES, LANES), 0)
    lane = lax.broadcasted_iota(jnp.int32, (SUBLANES, LANES), 1)
    lo_half = sub < HALF
    ones = jnp.ones((LANES, LANES), BF16)

    def token(t):
        xt = x_ref[t]
        xr = pltpu.roll(xt, HALF, axis=0)
        x_lo = jnp.where(lo_half, xt, xr)
        x_hi = jnp.where(lo_half, xr, xt)
        tile = jnp.zeros((SUBLANES, LANES), F32)
        for p0 in range(0, n_pairs, PAIRS_PER_DOT):
            prods = []
            for p in range(p0, p0 + PAIRS_PER_DOT):
                lo, hi = _pair_rows(tbl, e_ref, p, t, lo_half)
                prods.append(lo * x_lo + hi * x_hi)
            sums = jnp.dot(jnp.concatenate(prods, axis=0).astype(BF16), ones,
                           preferred_element_type=F32)
            for k in range(PAIRS_PER_DOT):
                tile = jnp.where(lane == p0 + k, sums[k * SUBLANES:(k + 1) * SUBLANES], tile)
        tile = tile + pltpu.roll(tile, SUBLANES - 1, axis=0)
        tile = tile + pltpu.roll(tile, SUBLANES - 2, axis=0)
        w_ref[pl.ds(t, 1), :] = jnp.where(lane[0:1] < n_pairs, tile[0:1],
                                          pltpu.roll(tile, n_pairs, axis=1)[HALF:HALF + 1])

    for t in range(tb):
        token(t)
    w_ref[...] = g_ref[...] * jax.nn.gelu(w_ref[...])


def _peer_v_kernel(e_ref, w_ref, h_ref, tbl_hbm, g_ref, b_ref, o_ref, tbl, sem, *, tb, alpha):
    _load_table(tbl_hbm, tbl, sem)
    n_pairs = e_ref.shape[1] // 2
    n_rows = w_ref.shape[1]
    sub = lax.broadcasted_iota(jnp.int32, (SUBLANES, LANES), 0)
    lo_half = sub < HALF
    n_feat = SUBLANES * LANES
    r = lax.broadcasted_iota(jnp.int32, (SUBLANES, n_rows), 1)
    own_chunk = (HALF * (r % 2) + (r // 2) % HALF
                 == lax.broadcasted_iota(jnp.int32, (SUBLANES, n_rows), 0))

    def token(t):
        rows = [pltpu.bitcast(_pair_words(tbl, e_ref, p, t, lo_half), BF16) for p in range(n_pairs)]
        stacked = jnp.concatenate(rows, axis=0)
        w_row = jnp.broadcast_to(w_ref[pl.ds(t, 1), :], (SUBLANES, n_rows))
        w_mat = jnp.where(own_chunk, w_row, 0.0).astype(BF16)
        o_ref[t] = jnp.dot(w_mat, stacked, preferred_element_type=F32)

    for t in range(tb):
        token(t)

    y = alpha * h_ref[...] + o_ref[...]
    mu = jnp.sum(jnp.sum(y, axis=2, keepdims=True), axis=1, keepdims=True) / n_feat
    d = y - mu
    var = jnp.sum(jnp.sum(d * d, axis=2, keepdims=True), axis=1, keepdims=True) / n_feat
    o_ref[...] = d * lax.rsqrt(var + LN_EPS) * g_ref[...] + b_ref[...]


def _pack_table(tab):
    E, D = tab.shape
    bits = lax.bitcast_convert_type(tab.astype(BF16), jnp.uint16).astype(jnp.uint32)
    packed = bits[:, :D // 2] | (bits[:, D // 2:] << 16)
    return packed.reshape(E * (D // 2) // LANES, LANES)


def _peer_scratch(tbl_words):
    return [pltpu.VMEM((tbl_words.shape[0] + 2 * SUBLANES, LANES), jnp.uint32), pltpu.SemaphoreType.DMA]


def _peer_u(e_tok, h_tiles, g_tok, tbl_words, tb=PEER_TOKEN_BLOCK):
    T, n_slots = e_tok.shape
    return pl.pallas_call(
        functools.partial(_peer_u_kernel, tb=tb),
        out_shape=jax.ShapeDtypeStruct((T, n_slots), F32),
        grid=(T // tb,),
        in_specs=[
            pl.BlockSpec((tb, n_slots), lambda i: (i, 0), memory_space=pltpu.SMEM),
            pl.BlockSpec((tb, SUBLANES, LANES), lambda i: (i, 0, 0)),
            pl.BlockSpec((tb, n_slots), lambda i: (i, 0)),
            pl.BlockSpec(memory_space=pl.ANY),
        ],
        out_specs=pl.BlockSpec((tb, n_slots), lambda i: (i, 0)),
        scratch_shapes=_peer_scratch(tbl_words),
        compiler_params=_vmem_limit(48),
        name="peer_u",
    )(e_tok, h_tiles, g_tok, tbl_words)


def _peer_v(e_tok, w_rows, h_tiles, tbl_words, ln_g, ln_b, alpha, tb=PEER_TOKEN_BLOCK):
    T, n_slots = e_tok.shape
    n_rows = w_rows.shape[1]
    const = lambda i: (0, 0)
    return pl.pallas_call(
        functools.partial(_peer_v_kernel, tb=tb, alpha=alpha),
        out_shape=jax.ShapeDtypeStruct((T, SUBLANES, LANES), F32),
        grid=(T // tb,),
        in_specs=[
            pl.BlockSpec((tb, n_slots), lambda i: (i, 0), memory_space=pltpu.SMEM),
            pl.BlockSpec((tb, n_rows), lambda i: (i, 0)),
            pl.BlockSpec((tb, SUBLANES, LANES), lambda i: (i, 0, 0)),
            pl.BlockSpec(memory_space=pl.ANY),
            pl.BlockSpec((SUBLANES, LANES), const),
            pl.BlockSpec((SUBLANES, LANES), const),
        ],
        out_specs=pl.BlockSpec((tb, SUBLANES, LANES), lambda i: (i, 0, 0)),
        scratch_shapes=_peer_scratch(tbl_words),
        compiler_params=_vmem_limit(48),
        name="peer_v",
    )(e_tok, w_rows, h_tiles, tbl_words, ln_g, ln_b)


def kernel(x, w_in, lambda_q1, lambda_k1, lambda_q2, lambda_k2, diff_subln_g, rel_bias, w_branch_a, w_branch_b, w_out, ln1_g, ln1_b, peer_wq, peer_k1, peer_k2, peer_u, peer_v, ln2_g, ln2_b):
    B, S, D = x.shape
    T = B * S
    depth = w_in.shape[0]
    alpha = (2.0 * depth) ** 0.25
    tq = 256
    assert D == SUBLANES * LANES and S % tq == 0 and tq % CHUNK == 0

    sb_w = SB_HEADS * SB_HEAD_DIM
    df_w = DF_HEADS * 2 * DF_HEAD_DIM
    assert SB_HEAD_DIM == 64 and DF_HEAD_DIM == 64
    col_scale = jnp.ones((w_in.shape[2],), F32)
    col_scale = col_scale.at[:sb_w].set(0.125).at[3 * sb_w:3 * sb_w + df_w].set(0.125)

    r = jnp.arange(tq, dtype=jnp.int32)
    later = (r[:, None] > r[None, :]).astype(BF16)
    mext = jnp.concatenate([later, jnp.ones((tq, LANES), BF16)], axis=1)
    bias_tiles, far_bias = _df_bias_tiles(rel_bias, tq)
    row_words = D // 2 // LANES
    n_slots = PEER_HEADS * PEER_TOPK

    h = x.reshape(T, D)
    for l in range(depth):
        lam_init = 0.8 - 0.6 * math.exp(-0.3 * l)
        proj = _inproj(h, (w_in[l] * col_scale).astype(BF16))
        o_sb = _sb_attention(proj, mext, B, S, tq)
        lam_vec = jnp.stack([lambda_q1[l], lambda_k1[l], lambda_q2[l], lambda_k2[l]]).astype(F32)
        o_df = _df_attention(proj, lam_vec, bias_tiles, far_bias, diff_subln_g[l].reshape(1, -1).astype(F32),
                             B, S, tq, lam_init)
        h = _merge(o_sb, o_df, proj, h, w_branch_a[l].astype(BF16), w_branch_b[l].astype(BF16),
                   w_out[l].astype(BF16), ln1_g[l].reshape(1, D), ln1_b[l].reshape(1, D), alpha)
        e_t, g_t = _peer_select(h, peer_wq[l].astype(BF16), peer_k1[l].astype(BF16),
                                peer_k2[l].astype(BF16), row_words)
        e_tok = e_t.T
        h_tiles = h.reshape(T, SUBLANES, LANES)
        w_tok = _peer_u(e_tok, h_tiles, g_t.T, _pack_table(peer_u[l]))
        w_rows = jnp.repeat(w_tok.reshape(T, 2, n_slots // 2).swapaxes(1, 2).reshape(T, n_slots),
                            SUBLANES, axis=1)
        h = _peer_v(e_tok, w_rows, h_tiles, _pack_table(peer_v[l]), ln2_g[l].reshape(SUBLANES, LANES),
                    ln2_b[l].reshape(SUBLANES, LANES), alpha).reshape(T, D)
    return h.reshape(B, S, D)
```
